```python
import jax
import jax.numpy as jnp
from jax import lax

D_MODEL = 4096
BATCH = 4
SEQ = 4096
DEPTH = 1

MIX_WIDTH = D_MODEL
RET_WIDTH = MIX_WIDTH // 2
RET_HEAD_DIM = 256
N_RET_HEADS = RET_WIDTH // RET_HEAD_DIM
RET_CHUNK = 128
SWA_WIDTH = MIX_WIDTH - RET_WIDTH
SWA_HEAD_DIM = 64
N_SWA_HEADS = SWA_WIDTH // SWA_HEAD_DIM
N_SWA_KV_HEADS = N_SWA_HEADS // 8
SWA_KV_WIDTH = N_SWA_KV_HEADS * SWA_HEAD_DIM
WINDOW = 128
ROPE_THETA = 10000.0
IN_PROJ_WIDTH = 4 * RET_WIDTH + SWA_WIDTH + 2 * SWA_KV_WIDTH
N_EXPERTS = 32
TOP_K = 4
D_EXPERT = D_MODEL // 2
SWIGLU_LIMIT = 7.0
SWIGLU_ALPHA = 1.702
MOE_BLOCK = 512
NORM_EPS = 1e-5
NEG_INF = -1e30

kernel_name = 'hybrid_retention_swa_moe_block'


def rms_norm(x, gain):
    x32 = x.astype(jnp.float32)
    y = x32 * lax.rsqrt(jnp.mean(x32 * x32, axis=-1, keepdims=True) + NORM_EPS)
    return (y * gain.astype(jnp.float32)).astype(x.dtype)


def rotate(x, positions, inv_freq):
    ang = positions.astype(jnp.float32)[..., None] * inv_freq
    cos = jnp.cos(ang)[:, :, None, :]
    sin = jnp.sin(ang)[:, :, None, :]
    x1, x2 = jnp.split(x.astype(jnp.float32), 2, axis=-1)
    return jnp.concatenate([x1 * cos - x2 * sin, x2 * cos + x1 * sin], axis=-1).astype(x.dtype)


def retention(q, k, v, g, positions):
    B, S, H, dk = q.shape
    dv = v.shape[-1]
    C = RET_CHUNK
    N = S // C
    f32 = jnp.float32
    inv_freq = 1.0 / (ROPE_THETA ** jnp.linspace(0.0, 1.0, dk // 2, dtype=f32))
    q = rotate(q, positions, inv_freq)
    k = rotate(k, positions, inv_freq) * (dk ** -0.5)
    log_gamma = jnp.log(1.0 - 2.0 ** (-5.0 - jnp.arange(H, dtype=f32)))
    idx = jnp.arange(C, dtype=f32)
    rel = idx[:, None] - idx[None, :]
    decay = jnp.where(rel >= 0, jnp.exp(log_gamma[:, None, None] * jnp.maximum(rel, 0.0)), 0.0)
    xi = jnp.exp(log_gamma[:, None] * (idx + 1.0))
    zeta = jnp.exp(log_gamma[:, None] * (C - 1.0 - idx))
    chunk_decay = jnp.exp(log_gamma * C)[None, :, None, None]
    qc = q.reshape(B, N, C, H, dk)
    kc = k.reshape(B, N, C, H, dk)
    vc = v.reshape(B, N, C, H, dv)
    scores = jnp.einsum('bnihd,bnjhd->bnhij', qc, kc) * decay
    intra = jnp.einsum('bnhij,bnjhe->bnihe', scores, vc)
    kv = jnp.einsum('bnjhd,hj,bnjhe->nbhde', kc, zeta, vc)

    def step(state, kv_n):
        return state * chunk_decay + kv_n, state

    _, prev_state = lax.scan(step, jnp.zeros(kv.shape[1:], kv.dtype), kv)
    cross = jnp.einsum('bnihd,hi,nbhde->bnihe', qc, xi, prev_state)
    o = (intra + cross).astype(f32).reshape(B, S, H, dv)
    o = o * lax.rsqrt(jnp.mean(o * o, axis=-1, keepdims=True) + NORM_EPS)
    o = o * jax.nn.silu(g.astype(f32))
    return o.astype(v.dtype).reshape(B, S, H * dv)


def sliding_window_attention(q, k, v, q_gain, k_gain, sinks, positions):
    B, S, Hq, d = q.shape
    Hkv = k.shape[2]
    G = Hq // Hkv
    W = WINDOW
    NB = S // W
    f32 = jnp.float32
    q = rms_norm(q, q_gain)
    k = rms_norm(k, k_gain)
    inv_freq = ROPE_THETA ** (-jnp.arange(0, d, 2, dtype=f32) / d)
    q = rotate(q, positions, inv_freq)
    k = rotate(k, positions, inv_freq)
    qb = q.reshape(B, NB, W, Hkv, G, d)

    def with_prev(t):
        tb = t.reshape(B, NB, W, Hkv, d)
        prev = jnp.pad(tb, ((0, 0), (1, 0), (0, 0), (0, 0), (0, 0)))[:, :-1]
        return jnp.concatenate([prev, tb], axis=2)

    kw = with_prev(k)
    vw = with_prev(v)
    logits = jnp.einsum('bnqhgd,bnjhd->bnhgqj', qb, kw).astype(f32) * (d ** -0.5)
    blk = jnp.arange(NB)[:, None] * W
    qpos = blk + jnp.arange(W)[None, :]
    kpos = blk - W + jnp.arange(2 * W)[None, :]
    rel = qpos[:, :, None] - kpos[:, None, :]
    allowed = (rel >= 0) & (rel < W) & (kpos[:, None, :] >= 0)
    logits = jnp.where(allowed[None, :, None, None], logits, NEG_INF)
    sink = jnp.broadcast_to(sinks.astype(f32).reshape(1, 1, Hkv, G, 1, 1), logits.shape[:-1] + (1,))
    probs = jax.nn.softmax(jnp.concatenate([logits, sink], axis=-1), axis=-1)[..., :-1]
    out = jnp.einsum('bnhgqj,bnjhd->bnqhgd', probs.astype(v.dtype), vw)
    return out.reshape(B, S, Hq * d)


def moe(xn, w_router, b_router, w_in, b_in, w_out, b_out):
    T, D = xn.shape
    M = T * TOP_K
    logits = (xn @ w_router + b_router).astype(jnp.float32)
    top_val, top_idx = lax.top_k(logits, TOP_K)
    gates = jax.nn.softmax(top_val, axis=-1)
    e_flat = top_idx.reshape(-1).astype(jnp.int32)
    tok_flat = jnp.repeat(jnp.arange(T, dtype=jnp.int32), TOP_K)
    g_flat = gates.reshape(-1).astype(xn.dtype)
    order = jnp.argsort(e_flat)
    e_sorted = e_flat[order]
    tok_sorted = tok_flat[order]
    g_sorted = g_flat[order]
    counts = jnp.bincount(e_flat, length=N_EXPERTS)
    starts = jnp.cumsum(counts) - counts
    padded = (counts + MOE_BLOCK - 1) // MOE_BLOCK * MOE_BLOCK
    pad_end = jnp.cumsum(padded)
    pad_start = pad_end - padded
    rank = jnp.arange(M, dtype=jnp.int32) - starts[e_sorted]
    dest = pad_start[e_sorted] + rank
    n_blocks = -(-M // MOE_BLOCK) + N_EXPERTS
    P = n_blocks * MOE_BLOCK
    tok_buf = jnp.full((P,), T, jnp.int32).at[dest].set(tok_sorted)
    gate_buf = jnp.zeros((P,), xn.dtype).at[dest].set(g_sorted)
    block_start = jnp.arange(n_blocks, dtype=jnp.int32) * MOE_BLOCK
    block_expert = jnp.minimum(jnp.searchsorted(pad_end, block_start, side='right'), N_EXPERTS - 1)
    x_pad = jnp.concatenate([xn, jnp.zeros((1, D), xn.dtype)], axis=0)

    def body(y, blk):
        tok, g, e = blk
        xb = x_pad[tok]
        hmid = xb @ w_in[e] + b_in[e]
        h_glu = jnp.minimum(hmid[:, :D_EXPERT], SWIGLU_LIMIT)
        h_lin = jnp.clip(hmid[:, D_EXPERT:], -SWIGLU_LIMIT, SWIGLU_LIMIT)
        act = h_glu * jax.nn.sigmoid(SWIGLU_ALPHA * h_glu) * (h_lin + 1.0)
        o = act @ w_out[e] + b_out[e]
        return y.at[tok].add(o * g[:, None]), None

    y, _ = lax.scan(body, jnp.zeros((T + 1, D), xn.dtype),
                    (tok_buf.reshape(n_blocks, MOE_BLOCK), gate_buf.reshape(n_blocks, MOE_BLOCK), block_expert))
    return y[:T]


def setup_inputs(seed: int = 0) -> dict:
    key = jax.random.key(seed)
    ks = jax.random.split(key, 14)
    f32 = jnp.float32

    def normal(k, shape, scale):
        return jax.random.normal(k, shape, f32) * scale

    return {
        'x': normal(ks[0], (BATCH, SEQ, D_MODEL), 1.0),
        'positions': jnp.broadcast_to(jnp.arange(SEQ, dtype=jnp.int32), (BATCH, SEQ)),
        'attn_norm_gain': 1.0 + normal(ks[1], (DEPTH, D_MODEL), 0.02),
        'w_in_proj': normal(ks[2], (DEPTH, D_MODEL, IN_PROJ_WIDTH), D_MODEL ** -0.5),
        'swa_q_gain': 1.0 + normal(ks[3], (DEPTH, SWA_HEAD_DIM), 0.02),
        'swa_k_gain': 1.0 + normal(ks[4], (DEPTH, SWA_HEAD_DIM), 0.02),
        'swa_sinks': normal(ks[5], (DEPTH, N_SWA_HEADS), 0.5),
        'w_out_proj': normal(ks[6], (DEPTH, MIX_WIDTH, D_MODEL), MIX_WIDTH ** -0.5),
        'ffn_norm_gain': 1.0 + normal(ks[7], (DEPTH, D_MODEL), 0.02),
        'w_router': normal(ks[8], (DEPTH, D_MODEL, N_EXPERTS), D_MODEL ** -0.5),
        'b_router': normal(ks[9], (DEPTH, N_EXPERTS), 0.01),
        'w_expert_in': normal(ks[10], (DEPTH, N_EXPERTS, D_MODEL, 2 * D_EXPERT), D_MODEL ** -0.5),
        'b_expert_in': normal(ks[11], (DEPTH, N_EXPERTS, 2 * D_EXPERT), 0.01),
        'w_expert_out': normal(ks[12], (DEPTH, N_EXPERTS, D_EXPERT, D_MODEL), D_EXPERT ** -0.5),
        'b_expert_out': normal(ks[13], (DEPTH, N_EXPERTS, D_MODEL), 0.01),
    }


def reference(x, positions, attn_norm_gain, w_in_proj, swa_q_gain, swa_k_gain, swa_sinks,
              w_out_proj, ffn_norm_gain, w_router, b_router, w_expert_in, b_expert_in,
              w_expert_out, b_expert_out):
    B, S, D = x.shape
    R, SW, KV = RET_WIDTH, SWA_WIDTH, SWA_KV_WIDTH
    splits = (R, 2 * R, 3 * R, 4 * R, 4 * R + SW, 4 * R + SW + KV)
    h = x
    for layer in range(DEPTH):
        xn = rms_norm(h, attn_norm_gain[layer])
        proj = jnp.einsum('bsd,df->bsf', xn, w_in_proj[layer])
        r_q, r_k, r_v, r_g, s_q, s_k, s_v = jnp.split(proj, splits, axis=-1)
        ret = retention(r_q.reshape(B, S, N_RET_HEADS, -1), r_k.reshape(B, S, N_RET_HEADS, -1),
                        r_v.reshape(B, S, N_RET_HEADS, -1), r_g.reshape(B, S, N_RET_HEADS, -1),
                        positions)
        swa = sliding_window_attention(s_q.reshape(B, S, N_SWA_HEADS, SWA_HEAD_DIM),
                                       s_k.reshape(B, S, N_SWA_KV_HEADS, SWA_HEAD_DIM),
                                       s_v.reshape(B, S, N_SWA_KV_HEADS, SWA_HEAD_DIM),
                                       swa_q_gain[layer], swa_k_gain[layer], swa_sinks[layer],
                                       positions)
        mixed = jnp.concatenate([ret, swa], axis=-1)
        h = h + jnp.einsum('bsm,md->bsd', mixed, w_out_proj[layer])
        hn = rms_norm(h, ffn_norm_gain[layer]).reshape(B * S, D)
        h = h + moe(hn, w_router[layer], b_router[layer], w_expert_in[layer], b_expert_in[layer],
                    w_expert_out[layer], b_expert_out[layer]).reshape(B, S, D)
    return h
```

```python
import functools

import jax
import jax.numpy as jnp
from jax import lax
from jax.experimental import pallas as pl
from jax.experimental.pallas import tpu as pltpu

RET_HEAD_DIM = 256
SWA_HEAD_DIM = 64
SWA_GROUP = 8
WINDOW = 128
ROPE_THETA = 10000.0
TOP_K = 4
SWIGLU_LIMIT = 7.0
SWIGLU_ALPHA = 1.702
NORM_EPS = 1e-5
NEG_INF = -1e30

RET_CHUNK = 256
LANES = 128
VMEM_LIMIT = 52 * 1024 * 1024

f32 = jnp.float32
bf16 = jnp.bfloat16


def _cparams(sem, vmem=VMEM_LIMIT):
    return pltpu.CompilerParams(dimension_semantics=sem, vmem_limit_bytes=vmem)


def _rope_kernel(pos_ref, fr_ref, sr_ref, fs_ref, ss_ref, cr_ref, snr_ref, cs_ref, sns_ref):
    pos = pos_ref[...].astype(f32)
    ang_r = pos * fr_ref[...]
    cr_ref[...] = jnp.cos(ang_r)
    snr_ref[...] = jnp.sin(ang_r) * sr_ref[...]
    ang_s = pos * fs_ref[...]
    cs_ref[...] = jnp.cos(ang_s)
    sns_ref[...] = jnp.sin(ang_s) * ss_ref[...]


def _rope_tables(pos_flat):
    T = pos_flat.shape[0]
    half = RET_HEAD_DIM // 2
    inv_r = 1.0 / (ROPE_THETA ** jnp.linspace(0.0, 1.0, half, dtype=f32))
    fr = jnp.concatenate([inv_r, inv_r])[None, :]
    sr = jnp.concatenate([-jnp.ones((half,), f32), jnp.ones((half,), f32)])[None, :]
    d = SWA_HEAD_DIM
    inv_s = ROPE_THETA ** (-jnp.arange(0, d, 2, dtype=f32) / d)
    fs = jnp.tile(inv_s, LANES // (d // 2))[None, :]
    sgn = jnp.concatenate([-jnp.ones((d // 2,), f32), jnp.ones((d // 2,), f32)])
    ss = jnp.tile(sgn, LANES // d)[None, :]
    tr = min(512, T)
    row = lambda w: pl.BlockSpec((tr, w), lambda i: (i, 0))
    cst = lambda w: pl.BlockSpec((1, w), lambda i: (0, 0))
    return pl.pallas_call(
        _rope_kernel,
        grid=(T // tr,),
        in_specs=[row(1), cst(RET_HEAD_DIM), cst(RET_HEAD_DIM), cst(LANES), cst(LANES)],
        out_specs=[row(RET_HEAD_DIM), row(RET_HEAD_DIM), row(LANES), row(LANES)],
        out_shape=[jax.ShapeDtypeStruct((T, RET_HEAD_DIM), f32)] * 2 + [jax.ShapeDtypeStruct((T, LANES), f32)] * 2,
        compiler_params=_cparams(("parallel",)),
        name="rope_tables",
    )(pos_flat, fr, sr, fs, ss)


def _rmsnorm_kernel(x_ref, g_ref, o_ref):
    x = x_ref[...]
    y = x * lax.rsqrt(jnp.mean(x * x, axis=-1, keepdims=True) + NORM_EPS)
    o_ref[...] = (y * g_ref[...]).astype(o_ref.dtype)


def _rmsnorm_bf16(x2d, gain):
    T, D = x2d.shape
    tr = min(256, T)
    return pl.pallas_call(
        _rmsnorm_kernel,
        grid=(T // tr,),
        in_specs=[pl.BlockSpec((tr, D), lambda i: (i, 0)), pl.BlockSpec((1, D), lambda i: (0, 0))],
        out_specs=pl.BlockSpec((tr, D), lambda i: (i, 0)),
        out_shape=jax.ShapeDtypeStruct((T, D), bf16),
        compiler_params=_cparams(("parallel",)),
        name="attn_rmsnorm",
    )(x2d, gain[None, :])


def _matmul_kernel(x_ref, w_ref, o_ref):
    o_ref[...] = jnp.dot(x_ref[...], w_ref[...], preferred_element_type=f32).astype(o_ref.dtype)


def _pick(n, prefs):
    for p in prefs:
        if n % p == 0:
            return p
    return n


def _in_proj(xn, w):
    T, D = xn.shape
    N = w.shape[1]
    tm = _pick(T, (1024, 512, 256))
    tn = _pick(N, (512, 256, 128))
    return pl.pallas_call(
        _matmul_kernel,
        grid=(T // tm, N // tn),
        in_specs=[pl.BlockSpec((tm, D), lambda i, j: (i, 0)), pl.BlockSpec((D, tn), lambda i, j: (0, j))],
        out_specs=pl.BlockSpec((tm, tn), lambda i, j: (i, j)),
        out_shape=jax.ShapeDtypeStruct((T, N), bf16),
        compiler_params=_cparams(("parallel", "arbitrary")),
        name="in_proj",
    )(xn, w)


def _retention_kernel(q_ref, k_ref, v_ref, g_ref, cos_ref, sin_ref, dec_ref, xi_ref, zeta_ref, cd_ref,
                      o_ref, state_ref):
    n = pl.program_id(2)

    @pl.when(n == 0)
    def _():
        state_ref[...] = jnp.zeros_like(state_ref)

    cos = cos_ref[...]
    sin = sin_ref[...]
    half = RET_HEAD_DIM // 2

    def rot(x):
        return x * cos + pltpu.roll(x, half, 1) * sin

    qr = rot(q_ref[...].astype(f32))
    kr = rot(k_ref[...].astype(f32)) * (RET_HEAD_DIM ** -0.5)
    v = v_ref[...]
    qb = qr.astype(bf16)
    scores = lax.dot_general(qb, kr.astype(bf16), (((1,), (1,)), ((), ())), preferred_element_type=f32)
    scores = scores * dec_ref[0]
    intra = jnp.dot(scores.astype(bf16), v, preferred_element_type=f32)
    state = state_ref[...]
    cross = jnp.dot((qr * xi_ref[0]).astype(bf16), state.astype(bf16), preferred_element_type=f32)
    kz = (kr * zeta_ref[0]).astype(bf16)
    kv = lax.dot_general(kz, v, (((0,), (0,)), ((), ())), preferred_element_type=f32)
    state_ref[...] = state * cd_ref[0] + kv
    o = intra + cross
    o = o * lax.rsqrt(jnp.mean(o * o, axis=-1, keepdims=True) + NORM_EPS)
    g = g_ref[...].astype(f32)
    o_ref[...] = (o * (g * jax.nn.sigmoid(g))).astype(o_ref.dtype)


def _retention(proj, cos_r, sin_r, B, S, n_heads):
    T = proj.shape[0]
    C = min(RET_CHUNK, S)
    NC = S // C
    hd = RET_HEAD_DIM
    log_gamma = jnp.log(1.0 - 2.0 ** (-5.0 - jnp.arange(n_heads, dtype=f32)))
    idx = jnp.arange(C, dtype=f32)
    rel = idx[:, None] - idx[None, :]
    decay = jnp.where(rel >= 0, jnp.exp(log_gamma[:, None, None] * jnp.maximum(rel, 0.0)), 0.0)
    xi = jnp.exp(log_gamma[:, None] * (idx + 1.0))[:, :, None]
    zeta = jnp.exp(log_gamma[:, None] * (C - 1.0 - idx))[:, :, None]
    cdec = jnp.exp(log_gamma * C)[:, None, None]

    def col(off):
        return pl.BlockSpec((C, hd), lambda b, h, n: (b * NC + n, off + h))

    tab = pl.BlockSpec((C, hd), lambda b, h, n: (b * NC + n, 0))
    per_head = lambda s: pl.BlockSpec((1,) + s, lambda b, h, n: (h, 0, 0))
    return pl.pallas_call(
        _retention_kernel,
        grid=(B, n_heads, NC),
        in_specs=[col(0), col(n_heads), col(2 * n_heads), col(3 * n_heads), tab, tab,
                  per_head((C, C)), per_head((C, 1)), per_head((C, 1)), per_head((1, 1))],
        out_specs=pl.BlockSpec((C, hd), lambda b, h, n: (b * NC + n, h)),
        out_shape=jax.ShapeDtypeStruct((T, n_heads * hd), bf16),
        scratch_shapes=[pltpu.VMEM((hd, hd), f32)],
        compiler_params=_cparams(("parallel", "parallel", "arbitrary")),
        name="retention",
    )(proj, proj, proj, proj, cos_r, sin_r, decay, xi, zeta, cdec)


def _swap32(x):
    w = x.shape[-1]
    lane = lax.broadcasted_iota(jnp.int32, x.shape, 1)
    first = (lane % SWA_HEAD_DIM) < (SWA_HEAD_DIM // 2)
    return jnp.where(first, pltpu.roll(x, w - SWA_HEAD_DIM // 2, 1), pltpu.roll(x, SWA_HEAD_DIM // 2, 1))


def _swa_kernel(sink_ref, q_ref, kvp_ref, kvc_ref, cosp_ref, sinp_ref, cosc_ref, sinc_ref,
                qg_ref, kg_ref, gmat_ref, o_ref):
    n = pl.program_id(1)
    g = pl.program_id(2)
    W = WINDOW
    d = SWA_HEAD_DIM
    n_pairs = q_ref.shape[1] // LANES

    q = q_ref[...].astype(f32)
    ssq = jnp.dot((q * q).astype(bf16), gmat_ref[...], preferred_element_type=f32)
    qn = q * lax.rsqrt(ssq * (1.0 / d) + NORM_EPS) * qg_ref[...]
    cosc = cosc_ref[...]
    sinc = sinc_ref[...]
    cosq = jnp.concatenate([cosc] * n_pairs, axis=1)
    sinq = jnp.concatenate([sinc] * n_pairs, axis=1)
    qr = (qn * cosq + _swap32(qn) * sinq) * (d ** -0.5)

    lane = lax.broadcasted_iota(jnp.int32, (W, LANES), 1)
    is_lo = lane < d

    def prep_kv(kv_ref, cos, sin):
        kv = kv_ref[...].astype(f32)
        ssk = jnp.sum(jnp.where(is_lo, kv * kv, 0.0), axis=-1, keepdims=True)
        kn = kv * lax.rsqrt(ssk * (1.0 / d) + NORM_EPS) * kg_ref[...]
        kr = kn * cos + _swap32(kn) * sin
        kk = jnp.where(is_lo, kr, pltpu.roll(kr, d, 1))
        v_lo = jnp.where(is_lo, pltpu.roll(kv, d, 1), 0.0)
        v_hi = jnp.where(is_lo, 0.0, kv)
        return kk.astype(bf16), v_lo.astype(bf16), v_hi.astype(bf16)

    kk_p, vlo_p, vhi_p = prep_kv(kvp_ref, cosp_ref[...], sinp_ref[...])
    kk_c, vlo_c, vhi_c = prep_kv(kvc_ref, cosc, sinc)
    kk = jnp.concatenate([kk_p, kk_c], axis=0)
    v_lo = jnp.concatenate([vlo_p, vlo_c], axis=0)
    v_hi = jnp.concatenate([vhi_p, vhi_c], axis=0)

    qi = lax.broadcasted_iota(jnp.int32, (2 * W, 2 * W), 0) % W
    kj = lax.broadcasted_iota(jnp.int32, (2 * W, 2 * W), 1)
    delta = kj - qi
    allowed = (delta >= 1) & (delta <= W) & ((kj >= W) | (n > 0))
    top = lax.broadcasted_iota(jnp.int32, (2 * W, 1), 0) < W

    for p in range(n_pairs):
        qp = qr[:, p * LANES:(p + 1) * LANES]
        lhs = jnp.concatenate([jnp.where(is_lo, qp, 0.0), jnp.where(is_lo, 0.0, qp)], axis=0).astype(bf16)
        logits = lax.dot_general(lhs, kk, (((1,), (1,)), ((), ())), preferred_element_type=f32)
        logits = jnp.where(allowed, logits, NEG_INF)
        h0 = g * SWA_GROUP + 2 * p
        sink = jnp.where(top, sink_ref[h0], sink_ref[h0 + 1])
        m = jnp.maximum(jnp.max(logits, axis=-1, keepdims=True), sink)
        e = jnp.exp(logits - m)
        den = jnp.sum(e, axis=-1, keepdims=True) + jnp.exp(sink - m)
        eb = e.astype(bf16)
        rden = 1.0 / den
        o_lo = jnp.dot(eb[:W], v_lo, preferred_element_type=f32) * rden[:W]
        o_hi = jnp.dot(eb[W:], v_hi, preferred_element_type=f32) * rden[W:]
        o_ref[:, p * LANES:(p + 1) * LANES] = (o_lo + o_hi).astype(o_ref.dtype)


def _swa(proj, cos_s, sin_s, sinks, q_gain, k_gain, B, S, q_off, kv_off, n_heads):
    T = proj.shape[0]
    W = WINDOW
    NB = S // W
    n_kv = n_heads // SWA_GROUP
    gw = SWA_GROUP * SWA_HEAD_DIM
    qg = jnp.tile(q_gain.astype(f32), SWA_GROUP)[None, :]
    kg = jnp.tile(k_gain.astype(f32), LANES // SWA_HEAD_DIM)[None, :]
    head_id = jnp.arange(gw) // SWA_HEAD_DIM
    gmat = (head_id[:, None] == head_id[None, :]).astype(bf16)
    qb, kvb = q_off // gw, kv_off // LANES

    def im(f):
        return lambda b, n, g, s: f(b, n, g)

    cur = lambda w: pl.BlockSpec((W, w), im(lambda b, n, g: (b * NB + n, 0)))
    prev = lambda w: pl.BlockSpec((W, w), im(lambda b, n, g: (b * NB + jnp.maximum(n - 1, 0), 0)))
    cst = lambda r, w: pl.BlockSpec((r, w), im(lambda b, n, g: (0, 0)))
    grid_spec = pltpu.PrefetchScalarGridSpec(
        num_scalar_prefetch=1,
        grid=(B, NB, n_kv),
        in_specs=[
            pl.BlockSpec((W, gw), im(lambda b, n, g: (b * NB + n, qb + g))),
            pl.BlockSpec((W, LANES), im(lambda b, n, g: (b * NB + jnp.maximum(n - 1, 0), kvb + g))),
            pl.BlockSpec((W, LANES), im(lambda b, n, g: (b * NB + n, kvb + g))),
            prev(LANES), prev(LANES), cur(LANES), cur(LANES),
            cst(1, gw), cst(1, LANES), cst(gw, gw),
        ],
        out_specs=pl.BlockSpec((W, gw), im(lambda b, n, g: (b * NB + n, g))),
    )
    return pl.pallas_call(
        _swa_kernel,
        grid_spec=grid_spec,
        out_shape=jax.ShapeDtypeStruct((T, n_heads * SWA_HEAD_DIM), bf16),
        compiler_params=_cparams(("parallel", "parallel", "parallel")),
        name="swa",
    )(sinks.astype(f32), proj, proj, proj, cos_s, sin_s, cos_s, sin_s, qg, kg, gmat)


def _out_proj_kernel(x_ref, a_ref, b_ref, wa_ref, wb_ref, o_ref):
    acc = jnp.dot(a_ref[...], wa_ref[...], preferred_element_type=f32)
    acc = acc + jnp.dot(b_ref[...], wb_ref[...], preferred_element_type=f32)
    o_ref[...] = x_ref[...] + acc


def _out_proj(x2d, ret, swa, w_ret, w_swa):
    T, D = x2d.shape
    R, SW = ret.shape[1], swa.shape[1]
    tm = _pick(T, (1024, 512, 256))
    tn = _pick(D, (512, 256, 128))
    return pl.pallas_call(
        _out_proj_kernel,
        grid=(T // tm, D // tn),
        in_specs=[pl.BlockSpec((tm, tn), lambda i, j: (i, j)),
                  pl.BlockSpec((tm, R), lambda i, j: (i, 0)),
                  pl.BlockSpec((tm, SW), lambda i, j: (i, 0)),
                  pl.BlockSpec((R, tn), lambda i, j: (0, j)),
                  pl.BlockSpec((SW, tn), lambda i, j: (0, j))],
        out_specs=pl.BlockSpec((tm, tn), lambda i, j: (i, j)),
        out_shape=jax.ShapeDtypeStruct((T, D), f32),
        compiler_params=_cparams(("parallel", "arbitrary")),
        name="out_proj",
    )(x2d, ret, swa, w_ret, w_swa)


def _router_kernel(h_ref, g_ref, wr_ref, br_ref, hp_ref, idx_ref, gate_ref):
    h = h_ref[...]
    hn = h * lax.rsqrt(jnp.mean(h * h, axis=-1, keepdims=True) + NORM_EPS) * g_ref[...]
    logits = jnp.dot(hn, wr_ref[...], preferred_element_type=f32, precision=lax.Precision.HIGHEST) + br_ref[...]
    E = logits.shape[-1]
    lane = lax.broadcasted_iota(jnp.int32, logits.shape, 1)
    vals, idxs = [], []
    cur = logits
    for _ in range(TOP_K):
        m = jnp.max(cur, axis=-1, keepdims=True)
        i = jnp.min(jnp.where(cur == m, lane, E), axis=-1, keepdims=True)
        vals.append(m)
        idxs.append(i)
        cur = jnp.where(lane == i, -jnp.inf, cur)
    ex = [jnp.exp(v - vals[0]) for v in vals]
    tot = ex[0]
    for e in ex[1:]:
        tot = tot + e
    col = lax.broadcasted_iota(jnp.int32, idx_ref.shape, 1)
    idx_out = jnp.zeros(idx_ref.shape, jnp.int32)
    gate_out = jnp.zeros(gate_ref.shape, f32)
    for k in range(TOP_K):
        idx_out = jnp.where(col == k, idxs[k], idx_out)
        gate_out = jnp.where(col == k, ex[k] / tot, gate_out)
    idx_ref[...] = idx_out
    gate_ref[...] = gate_out
    half = hn.shape[1] // 2
    lo = lax.bitcast_convert_type(hn[:, :half].astype(bf16).astype(f32), jnp.uint32)
    hi = lax.bitcast_convert_type(hn[:, half:].astype(bf16).astype(f32), jnp.uint32)
    hp_ref[...] = hi | (lo >> 16)


def _router(h2d, gain, w_router, b_router):
    T, D = h2d.shape
    E = w_router.shape[1]
    tr = min(256, T)
    row = lambda w: pl.BlockSpec((tr, w), lambda i: (i, 0))
    return pl.pallas_call(
        _router_kernel,
        grid=(T // tr,),
        in_specs=[row(D), pl.BlockSpec((1, D), lambda i: (0, 0)),
                  pl.BlockSpec((D, E), lambda i: (0, 0)), pl.BlockSpec((1, E), lambda i: (0, 0))],
        out_specs=[row(D // 2), row(TOP_K), row(TOP_K)],
        out_shape=[jax.ShapeDtypeStruct((T, D // 2), jnp.uint32),
                   jax.ShapeDtypeStruct((T, TOP_K), jnp.int32),
                   jax.ShapeDtypeStruct((T, TOP_K), f32)],
        compiler_params=_cparams(("parallel",)),
        name="ffn_norm_router",
    )(h2d, gain[None, :], w_router, b_router[None, :])


def _moe_kernel(te_ref, tr_ref, rowtok_hbm, hp_hbm, wg_ref, wl_ref, wo_ref, bg_ref, bl_ref, bo_ref, gate_ref,
                o_ref, xg_ref, xb_ref, act_ref, idx_ref, idx_sem, row_sem, *, n_steps_in):
    i = pl.program_id(0)
    j = pl.program_id(1)
    n_tiles = pl.num_programs(0)
    tm, half = xg_ref.shape
    tn1 = wg_ref.shape[2]
    rows = tr_ref[i]

    def idx_copy(t):
        return pltpu.make_async_copy(rowtok_hbm.at[t], idx_ref, idx_sem)

    def gather_start():
        def body(r, c):
            pltpu.make_async_copy(hp_hbm.at[idx_ref[r]], xg_ref.at[r], row_sem).start()
            return c
        lax.fori_loop(0, tm, body, 0)

    def gather_wait():
        pltpu.make_async_copy(hp_hbm.at[pl.ds(0, tm)], xg_ref, row_sem).wait()

    @pl.when((j == 0) & (rows > 0))
    def _():
        @pl.when(i == 0)
        def _():
            idx_copy(0).start()
            idx_copy(0).wait()
            gather_start()

        gather_wait()
        w = xg_ref[...]
        xb_ref[:, :half] = lax.bitcast_convert_type(w << 16, f32).astype(bf16)
        xb_ref[:, half:] = lax.bitcast_convert_type(w & jnp.uint32(0xFFFF0000), f32).astype(bf16)

    nxt = jnp.minimum(i + 1, n_tiles - 1)
    has_next = (i + 1 < n_tiles) & (tr_ref[nxt] > 0)

    @pl.when((j == 0) & has_next)
    def _():
        idx_copy(nxt).start()

    @pl.when((j == 1) & has_next)
    def _():
        idx_copy(nxt).wait()
        gather_start()

    @pl.when((j < n_steps_in) & (rows > 0))
    def _():
        x = xb_ref[...]
        hg = jnp.dot(x, wg_ref[0].astype(bf16), preferred_element_type=f32) + bg_ref[0]
        hl = jnp.dot(x, wl_ref[0].astype(bf16), preferred_element_type=f32) + bl_ref[0]
        hg = jnp.minimum(hg, SWIGLU_LIMIT)
        hl = jnp.clip(hl, -SWIGLU_LIMIT, SWIGLU_LIMIT)
        a = (hg * jax.nn.sigmoid(SWIGLU_ALPHA * hg) * (hl + 1.0)).astype(bf16)
        for jj in range(n_steps_in):
            @pl.when(j == jj)
            def _():
                act_ref[:, jj * tn1:(jj + 1) * tn1] = a

    @pl.when((j >= n_steps_in) & (rows > 0))
    def _():
        o = jnp.dot(act_ref[...], wo_ref[0].astype(bf16), preferred_element_type=f32) + bo_ref[0]
        o_ref[...] = o * gate_ref[...]

    @pl.when((j >= n_steps_in) & (rows == 0))
    def _():
        o_ref[...] = jnp.zeros_like(o_ref)


def _moe_tile_rows(F):
    return 512


def _moe(hp, tile_expert, tile_rows, row_tok, row_gate, w_in, b_in, w_out, b_out):
    T, half = hp.shape
    D = 2 * half
    E, F = w_out.shape[0], w_out.shape[1]
    n_tiles, tm = row_tok.shape
    P = n_tiles * tm
    tn1 = _pick(F, (256, 128))
    tn2 = _pick(D, (512, 256, 128))
    J1, J2 = F // tn1, D // tn2

    def im(f):
        return lambda i, j, te, tr: f(i, j, te)

    grid_spec = pltpu.PrefetchScalarGridSpec(
        num_scalar_prefetch=2,
        grid=(n_tiles, J1 + J2),
        in_specs=[
            pl.BlockSpec(memory_space=pl.ANY),
            pl.BlockSpec(memory_space=pl.ANY),
            pl.BlockSpec((1, D, tn1), im(lambda i, j, te: (te[i], 0, jnp.minimum(j, J1 - 1)))),
            pl.BlockSpec((1, D, tn1), im(lambda i, j, te: (te[i], 0, J1 + jnp.minimum(j, J1 - 1)))),
            pl.BlockSpec((1, F, tn2), im(lambda i, j, te: (te[i], 0, jnp.maximum(j - J1, 0)))),
            pl.BlockSpec((1, 1, tn1), im(lambda i, j, te: (te[i], 0, jnp.minimum(j, J1 - 1)))),
            pl.BlockSpec((1, 1, tn1), im(lambda i, j, te: (te[i], 0, J1 + jnp.minimum(j, J1 - 1)))),
            pl.BlockSpec((1, 1, tn2), im(lambda i, j, te: (te[i], 0, jnp.maximum(j - J1, 0)))),
            pl.BlockSpec((tm, 1), im(lambda i, j, te: (i, 0))),
        ],
        out_specs=pl.BlockSpec((tm, tn2), im(lambda i, j, te: (i, jnp.maximum(j - J1, 0)))),
        scratch_shapes=[
            pltpu.VMEM((tm, half), jnp.uint32),
            pltpu.VMEM((tm, D), bf16),
            pltpu.VMEM((tm, F), bf16),
            pltpu.SMEM((tm,), jnp.int32),
            pltpu.SemaphoreType.DMA,
            pltpu.SemaphoreType.DMA,
        ],
    )
    return pl.pallas_call(
        functools.partial(_moe_kernel, n_steps_in=J1),
        grid_spec=grid_spec,
        out_shape=jax.ShapeDtypeStruct((P, D), f32),
        compiler_params=_cparams(("arbitrary", "arbitrary")),
        name="moe_experts",
    )(tile_expert, tile_rows, row_tok, hp, w_in, w_in, w_out,
      b_in[:, None, :], b_in[:, None, :], b_out[:, None, :], row_gate)


def _combine_kernel(pos_hbm, o_hbm, h_ref, out_ref, buf_ref, idx_ref, idx_sem, row_sem):
    i = pl.program_id(0)
    n = pl.num_programs(0)
    tq = h_ref.shape[0]
    slot = i % 2

    def idx_copy(t, s):
        return pltpu.make_async_copy(pos_hbm.at[t], idx_ref.at[s], idx_sem.at[s])

    def gather_start(s):
        def body(q, c):
            for k in range(TOP_K):
                pltpu.make_async_copy(o_hbm.at[idx_ref[s, q * TOP_K + k]], buf_ref.at[s, k * tq + q],
                                      row_sem.at[s]).start()
            return c
        lax.fori_loop(0, tq, body, 0)

    @pl.when(i == 0)
    def _():
        idx_copy(0, 0).start()
        idx_copy(0, 0).wait()
        gather_start(0)

    @pl.when(i + 1 < n)
    def _():
        idx_copy(i + 1, 1 - slot).start()
        idx_copy(i + 1, 1 - slot).wait()
        gather_start(1 - slot)

    pltpu.make_async_copy(o_hbm.at[pl.ds(0, TOP_K * tq)], buf_ref.at[slot], row_sem.at[slot]).wait()
    acc = h_ref[...]
    for k in range(TOP_K):
        acc = acc + buf_ref[slot, k * tq:(k + 1) * tq, :]
    out_ref[...] = acc


def _combine(h2d, o_sorted, pos):
    T, D = h2d.shape
    tq = min(128, T)
    pos2 = pos.reshape(T // tq, tq * TOP_K)
    return pl.pallas_call(
        _combine_kernel,
        grid=(T // tq,),
        in_specs=[pl.BlockSpec(memory_space=pl.ANY), pl.BlockSpec(memory_space=pl.ANY),
                  pl.BlockSpec((tq, D), lambda i: (i, 0))],
        out_specs=pl.BlockSpec((tq, D), lambda i: (i, 0)),
        out_shape=jax.ShapeDtypeStruct((T, D), f32),
        scratch_shapes=[pltpu.VMEM((2, TOP_K * tq, D), f32),
                        pltpu.SMEM((2, TOP_K * tq), jnp.int32),
                        pltpu.SemaphoreType.DMA((2,)),
                        pltpu.SemaphoreType.DMA((2,))],
        compiler_params=_cparams(("arbitrary",)),
        name="moe_combine",
    )(pos2, o_sorted, h2d)


def _routing_tables(top_idx, gates, n_experts, tm):
    T = top_idx.shape[0]
    M = T * TOP_K
    e_flat = top_idx.reshape(-1)
    order = jnp.argsort(e_flat, stable=True).astype(jnp.int32)
    e_sorted = e_flat[order]
    tok_sorted = order // TOP_K
    counts = jnp.sum((e_flat[:, None] == jnp.arange(n_experts, dtype=jnp.int32)[None, :]).astype(jnp.int32), axis=0)
    starts = jnp.cumsum(counts) - counts
    tiles_per = (counts + tm - 1) // tm
    tile_end = jnp.cumsum(tiles_per)
    tile_start = tile_end - tiles_per
    rank = jnp.arange(M, dtype=jnp.int32) - starts[e_sorted]
    dest = tile_start[e_sorted] * tm + rank
    n_tiles = M // tm + n_experts
    P = n_tiles * tm
    row_tok = jnp.zeros((P,), jnp.int32).at[dest].set(tok_sorted)
    row_gate = jnp.zeros((P,), f32).at[dest].set(gates.reshape(-1)[order])
    pos = jnp.zeros((M,), jnp.int32).at[order].set(dest)
    t_ids = jnp.arange(n_tiles, dtype=jnp.int32)
    n_used = tile_end[-1]
    te = jnp.minimum(jnp.searchsorted(tile_end, t_ids, side='right'), n_experts - 1).astype(jnp.int32)
    last_used = te[jnp.maximum(n_used - 1, 0)]
    used = t_ids < n_used
    tile_expert = jnp.where(used, te, last_used)
    tile_rows = jnp.where(used, jnp.clip(counts[te] - (t_ids - tile_start[te]) * tm, 0, tm), 0).astype(jnp.int32)
    return tile_expert, tile_rows, row_tok.reshape(n_tiles, tm), row_gate[:, None], pos.reshape(T, TOP_K)


def kernel(x, positions, attn_norm_gain, w_in_proj, swa_q_gain, swa_k_gain, swa_sinks, w_out_proj, ffn_norm_gain,
           w_router, b_router, w_expert_in, b_expert_in, w_expert_out, b_expert_out):
    B, S, D = x.shape
    T = B * S
    depth = w_in_proj.shape[0]
    R = D // 2
    SW = D - R
    n_ret = R // RET_HEAD_DIM
    n_swa = swa_sinks.shape[1]
    KV = (n_swa // SWA_GROUP) * SWA_HEAD_DIM
    n_experts = w_router.shape[2]
    F = w_expert_out.shape[2]
    assert w_in_proj.shape[2] == 4 * R + SW + 2 * KV and SW == n_swa * SWA_HEAD_DIM
    assert S % WINDOW == 0 and KV % LANES == 0 or (2 * KV) % LANES == 0

    cos_r, sin_r, cos_s, sin_s = _rope_tables(positions.reshape(T, 1))
    h = x.reshape(T, D)
    for layer in range(depth):
        w_in = w_in_proj[layer]
        wk = w_in[:, 4 * R + SW:4 * R + SW + KV].reshape(D, -1, SWA_HEAD_DIM)
        wv = w_in[:, 4 * R + SW + KV:].reshape(D, -1, SWA_HEAD_DIM)
        w_kv = jnp.concatenate([wk, wv], axis=2).reshape(D, 2 * KV)
        w_in_b = jnp.concatenate([w_in[:, :4 * R + SW], w_kv], axis=1).astype(bf16)
        w_out_b = w_out_proj[layer].astype(bf16)

        xn = _rmsnorm_bf16(h, attn_norm_gain[layer])
        proj = _in_proj(xn, w_in_b)
        ret = _retention(proj, cos_r, sin_r, B, S, n_ret)
        swa = _swa(proj, cos_s, sin_s, swa_sinks[layer], swa_q_gain[layer], swa_k_gain[layer],
                   B, S, 4 * R, 4 * R + SW, n_swa)
        h = _out_proj(h, ret, swa, w_out_b[:R], w_out_b[R:])

        hp, top_idx, gates = _router(h, ffn_norm_gain[layer], w_router[layer], b_router[layer])
        tm = _moe_tile_rows(F)
        tile_expert, tile_rows, row_tok, row_gate, pos = _routing_tables(top_idx, gates, n_experts, tm)
        o_sorted = _moe(hp, tile_expert, tile_rows, row_tok, row_gate,
                        w_expert_in[layer], b_expert_in[layer], w_expert_out[layer], b_expert_out[layer])
        h = _combine(h, o_sorted, pos)
    return h.reshape(B, S, D)
```

```python
import functools

import jax
import jax.numpy as jnp
from jax import lax
from jax.experimental import pallas as pl
from jax.experimental.pallas import tpu as pltpu

RET_HEAD_DIM = 256
SWA_HEAD_DIM = 64
SWA_GROUP = 8
WINDOW = 128
ROPE_THETA = 10000.0
TOP_K = 4
SWIGLU_LIMIT = 7.0
SWIGLU_ALPHA = 1.702
NORM_EPS = 1e-5
NEG_INF = -1e30

RET_CHUNK = 256
MOE_TILE = 1024
MOE_CHUNK = 512
LANES = 128
SUBLANES = 8
VMEM_LIMIT = 56 * 1024 * 1024

f32 = jnp.float32
bf16 = jnp.bfloat16
u32 = jnp.uint32


def _cparams(sem, vmem=VMEM_LIMIT):
    return pltpu.CompilerParams(dimension_semantics=sem, vmem_limit_bytes=vmem)


def _pick(n, prefs):
    for p in prefs:
        if n % p == 0:
            return p
    return n


def _pack_bf16_pair(lo, hi):
    lo_b = lax.bitcast_convert_type(lo.astype(bf16).astype(f32), u32)
    hi_b = lax.bitcast_convert_type(hi.astype(bf16).astype(f32), u32)
    return hi_b | (lo_b >> 16)


def _unpack_lo(w):
    return lax.bitcast_convert_type(w << 16, f32)


def _unpack_hi(w):
    return lax.bitcast_convert_type(w & jnp.uint32(0xFFFF0000), f32)


def _rope_kernel(pos_ref, fr_ref, fs_ref, ss_ref, cr_ref, snr_ref, cs_ref, sns_ref):
    pos = pos_ref[...].astype(f32)
    ang_r = pos * fr_ref[...]
    cr_ref[...] = jnp.cos(ang_r)
    snr_ref[...] = jnp.sin(ang_r)
    ang_s = pos * fs_ref[...]
    cs_ref[...] = jnp.cos(ang_s)
    sns_ref[...] = jnp.sin(ang_s) * ss_ref[...]


def _rope_tables(pos_flat):
    T = pos_flat.shape[0]
    half = RET_HEAD_DIM // 2
    assert half == LANES
    fr = (1.0 / (ROPE_THETA ** jnp.linspace(0.0, 1.0, half, dtype=f32)))[None, :]
    d = SWA_HEAD_DIM
    inv_s = ROPE_THETA ** (-jnp.arange(0, d, 2, dtype=f32) / d)
    fs = jnp.tile(inv_s, LANES // (d // 2))[None, :]
    sgn = jnp.concatenate([-jnp.ones((d // 2,), f32), jnp.ones((d // 2,), f32)])
    ss = jnp.tile(sgn, LANES // d)[None, :]
    tr = min(512, T)
    row = lambda w: pl.BlockSpec((tr, w), lambda i: (i, 0))
    cst = pl.BlockSpec((1, LANES), lambda i: (0, 0))
    return pl.pallas_call(
        _rope_kernel,
        grid=(T // tr,),
        in_specs=[row(1), cst, cst, cst],
        out_specs=[row(LANES)] * 4,
        out_shape=[jax.ShapeDtypeStruct((T, LANES), f32)] * 4,
        compiler_params=_cparams(("parallel",)),
        name="rope_tables",
    )(pos_flat, fr, fs, ss)


def _rmsnorm_kernel(x_ref, g_ref, o_ref):
    x = x_ref[...]
    y = x * lax.rsqrt(jnp.mean(x * x, axis=-1, keepdims=True) + NORM_EPS)
    o_ref[...] = (y * g_ref[...]).astype(o_ref.dtype)


def _rmsnorm_bf16(x2d, gain):
    T, D = x2d.shape
    tr = min(256, T)
    return pl.pallas_call(
        _rmsnorm_kernel,
        grid=(T // tr,),
        in_specs=[pl.BlockSpec((tr, D), lambda i: (i, 0)), pl.BlockSpec((1, D), lambda i: (0, 0))],
        out_specs=pl.BlockSpec((tr, D), lambda i: (i, 0)),
        out_shape=jax.ShapeDtypeStruct((T, D), bf16),
        compiler_params=_cparams(("parallel",)),
        name="attn_rmsnorm",
    )(x2d, gain[None, :])


def _matmul_kernel(x_ref, w_ref, o_ref):
    o_ref[...] = jnp.dot(x_ref[...], w_ref[...], preferred_element_type=f32).astype(o_ref.dtype)


def _in_proj(xn, w):
    T, D = xn.shape
    N = w.shape[1]
    tm = _pick(T, (1024, 512, 256))
    tn = _pick(N, (768, 512, 256, 128))
    return pl.pallas_call(
        _matmul_kernel,
        grid=(T // tm, N // tn),
        in_specs=[pl.BlockSpec((tm, D), lambda i, j: (i, 0)), pl.BlockSpec((D, tn), lambda i, j: (0, j))],
        out_specs=pl.BlockSpec((tm, tn), lambda i, j: (i, j)),
        out_shape=jax.ShapeDtypeStruct((T, N), bf16),
        compiler_params=_cparams(("parallel", "arbitrary")),
        name="in_proj",
    )(xn, w)


def _retention_kernel(q_ref, k_ref, v_ref, g_ref, cos_ref, sin_ref, dec_ref, xi_ref, zeta_ref, cd_ref,
                      o_ref, state_ref):
    n = pl.program_id(1)
    hd = RET_HEAD_DIM
    n_heads = q_ref.shape[1] // hd

    @pl.when(n == 0)
    def _():
        state_ref[...] = jnp.zeros_like(state_ref)

    c = cos_ref[...]
    s = sin_ref[...]
    cos = jnp.concatenate([c, c], axis=1)
    sin = jnp.concatenate([-s, s], axis=1)

    def rot(x):
        return x * cos + pltpu.roll(x, hd // 2, 1) * sin

    for h in range(n_heads):
        sl = slice(h * hd, (h + 1) * hd)
        qr = rot(q_ref[:, sl].astype(f32))
        kr = rot(k_ref[:, sl].astype(f32)) * (hd ** -0.5)
        v = v_ref[:, sl]
        scores = lax.dot_general(qr.astype(bf16), kr.astype(bf16), (((1,), (1,)), ((), ())),
                                 preferred_element_type=f32)
        scores = scores * dec_ref[h]
        intra = jnp.dot(scores.astype(bf16), v, preferred_element_type=f32)
        state = state_ref[h]
        cross = jnp.dot((qr * xi_ref[h]).astype(bf16), state.astype(bf16), preferred_element_type=f32)
        kz = (kr * zeta_ref[h]).astype(bf16)
        kv = lax.dot_general(kz, v, (((0,), (0,)), ((), ())), preferred_element_type=f32)
        state_ref[h] = state * cd_ref[h] + kv
        o = intra + cross
        o = o * lax.rsqrt(jnp.mean(o * o, axis=-1, keepdims=True) + NORM_EPS)
        g = g_ref[:, sl].astype(f32)
        o_ref[:, sl] = (o * (g * jax.nn.sigmoid(g))).astype(o_ref.dtype)


def _retention(proj, cos_r, sin_r, B, S, n_heads):
    T = proj.shape[0]
    C = min(RET_CHUNK, S)
    NC = S // C
    hd = RET_HEAD_DIM
    R = n_heads * hd
    log_gamma = jnp.log(1.0 - 2.0 ** (-5.0 - jnp.arange(n_heads, dtype=f32)))
    idx = jnp.arange(C, dtype=f32)
    rel = idx[:, None] - idx[None, :]
    decay = jnp.where(rel >= 0, jnp.exp(log_gamma[:, None, None] * jnp.maximum(rel, 0.0)), 0.0)
    xi = jnp.exp(log_gamma[:, None] * (idx + 1.0))[:, :, None]
    zeta = jnp.exp(log_gamma[:, None] * (C - 1.0 - idx))[:, :, None]
    cdec = jnp.exp(log_gamma * C)[:, None, None]

    col = lambda j: pl.BlockSpec((C, R), lambda b, n: (b * NC + n, j))
    tab = pl.BlockSpec((C, LANES), lambda b, n: (b * NC + n, 0))
    per_head = lambda s: pl.BlockSpec((n_heads,) + s, lambda b, n: (0, 0, 0))
    return pl.pallas_call(
        _retention_kernel,
        grid=(B, NC),
        in_specs=[col(0), col(1), col(2), col(3), tab, tab,
                  per_head((C, C)), per_head((C, 1)), per_head((C, 1)), per_head((1, 1))],
        out_specs=pl.BlockSpec((C, R), lambda b, n: (b * NC + n, 0)),
        out_shape=jax.ShapeDtypeStruct((T, R), bf16),
        scratch_shapes=[pltpu.VMEM((n_heads, hd, hd), f32)],
        compiler_params=_cparams(("parallel", "arbitrary")),
        name="retention",
    )(proj, proj, proj, proj, cos_r, sin_r, decay, xi, zeta, cdec)


def _swap32(x):
    w = x.shape[-1]
    lane = lax.broadcasted_iota(jnp.int32, x.shape, 1)
    first = (lane % SWA_HEAD_DIM) < (SWA_HEAD_DIM // 2)
    return jnp.where(first, pltpu.roll(x, w - SWA_HEAD_DIM // 2, 1), pltpu.roll(x, SWA_HEAD_DIM // 2, 1))


def _swa_kernel(sink_ref, q_ref, kp_ref, kc_ref, vp_ref, vc_ref, cosp_ref, sinp_ref, cosc_ref, sinc_ref,
                qg_ref, kg_ref, gmat_ref, o_ref):
    n = pl.program_id(1)
    W = WINDOW
    d = SWA_HEAD_DIM
    gw = SWA_GROUP * d
    kvw = kc_ref.shape[1]
    n_kv = kvw // d
    pairs = gw // LANES

    lane = lax.broadcasted_iota(jnp.int32, (W, LANES), 1)
    is_lo = lane < d
    cosc = cosc_ref[...]
    sinc = sinc_ref[...]

    def rms_rot(x, gain, gsub, cos, sin):
        reps = x.shape[1] // LANES
        ssq = jnp.dot((x * x).astype(bf16), gsub, preferred_element_type=f32)
        xn = x * lax.rsqrt(ssq * (1.0 / d) + NORM_EPS) * gain
        return xn * jnp.concatenate([cos] * reps, axis=1) + _swap32(xn) * jnp.concatenate([sin] * reps, axis=1)

    gk = gmat_ref[:kvw, :kvw]
    krp = rms_rot(kp_ref[...].astype(f32), kg_ref[...], gk, cosp_ref[...], sinp_ref[...])
    krc = rms_rot(kc_ref[...].astype(f32), kg_ref[...], gk, cosc, sinc)
    vp = vp_ref[...].astype(f32)
    vc = vc_ref[...].astype(f32)

    neg_prev = jnp.where(n > 0, 0.0, NEG_INF)
    row = lax.broadcasted_iota(jnp.int32, (2 * W, W), 0) % W
    colj = lax.broadcasted_iota(jnp.int32, (2 * W, W), 1)
    tri = colj <= row
    top = lax.broadcasted_iota(jnp.int32, (2 * W, 1), 0) < W

    for g in range(n_kv):
        c, hi_half = divmod(g, LANES // d)

        def halves(x):
            xc = x[:, c * LANES:(c + 1) * LANES]
            r = pltpu.roll(xc, d, 1)
            return (r, xc) if hi_half else (xc, r)

        def keys(x):
            a, b = halves(x)
            return jnp.where(is_lo, a, b)

        def vals(x):
            a, b = halves(x)
            v_lo = jnp.where(is_lo, a, jnp.where(lane == d, 1.0, 0.0))
            v_hi = jnp.where(is_lo, jnp.where(lane == 0, 1.0, 0.0), b)
            return v_lo, v_hi

        kk = jnp.concatenate([keys(krp), keys(krc)], axis=0).astype(bf16)
        vlo_p, vhi_p = vals(vp)
        vlo_c, vhi_c = vals(vc)
        v_lo = jnp.concatenate([vlo_p, vlo_c], axis=0).astype(bf16)
        v_hi = jnp.concatenate([vhi_p, vhi_c], axis=0).astype(bf16)

        q = q_ref[:, g * gw:(g + 1) * gw].astype(f32)
        qr = rms_rot(q, qg_ref[...], gmat_ref[...], cosc, sinc) * (d ** -0.5)

        for p in range(pairs):
            qp = qr[:, p * LANES:(p + 1) * LANES]
            lhs = jnp.concatenate([jnp.where(is_lo, qp, 0.0), jnp.where(is_lo, 0.0, qp)], axis=0).astype(bf16)
            lg = lax.dot_general(lhs, kk, (((1,), (1,)), ((), ())), preferred_element_type=f32)
            merged = jnp.where(tri, lg[:, W:], lg[:, :W] + neg_prev)
            h0 = g * SWA_GROUP + 2 * p
            sink = jnp.where(top, sink_ref[h0], sink_ref[h0 + 1])
            m = jnp.maximum(jnp.max(merged, axis=-1, keepdims=True), sink)
            eb = jnp.exp(merged - m).astype(bf16)
            zero = jnp.zeros_like(eb)
            probs = jnp.concatenate([jnp.where(tri, zero, eb), jnp.where(tri, eb, zero)], axis=1)
            o_lo = jnp.dot(probs[:W], v_lo, preferred_element_type=f32)
            o_hi = jnp.dot(probs[W:], v_hi, preferred_element_type=f32)
            esink = jnp.exp(sink - m)
            r_lo = 1.0 / (o_lo[:, d:d + 1] + esink[:W])
            r_hi = 1.0 / (o_hi[:, 0:1] + esink[W:])
            out = jnp.where(is_lo, o_lo * r_lo, o_hi * r_hi)
            o_ref[:, g * gw + p * LANES:g * gw + (p + 1) * LANES] = out.astype(o_ref.dtype)


def _swa(proj, cos_s, sin_s, sinks, q_gain, k_gain, B, S, q_off, n_heads):
    T = proj.shape[0]
    W = WINDOW
    NB = S // W
    d = SWA_HEAD_DIM
    n_kv = n_heads // SWA_GROUP
    SW, KV = n_heads * d, n_kv * d
    gw = SWA_GROUP * d
    assert q_off % SW == 0 and (q_off + SW) % KV == 0 and KV % LANES == 0 and KV <= gw
    qg = jnp.tile(q_gain.astype(f32), SWA_GROUP)[None, :]
    kg = jnp.tile(k_gain.astype(f32), n_kv)[None, :]
    head_id = jnp.arange(gw) // d
    gmat = (head_id[:, None] == head_id[None, :]).astype(bf16)
    qb = q_off // SW
    kb = (q_off + SW) // KV

    def im(f):
        return lambda b, n, s: f(b, n)

    cur = lambda w, j: pl.BlockSpec((W, w), im(lambda b, n: (b * NB + n, j)))
    prev = lambda w, j: pl.BlockSpec((W, w), im(lambda b, n: (b * NB + jnp.maximum(n - 1, 0), j)))
    cst = lambda r, w: pl.BlockSpec((r, w), im(lambda b, n: (0, 0)))
    grid_spec = pltpu.PrefetchScalarGridSpec(
        num_scalar_prefetch=1,
        grid=(B, NB),
        in_specs=[cur(SW, qb), prev(KV, kb), cur(KV, kb), prev(KV, kb + 1), cur(KV, kb + 1),
                  prev(LANES, 0), prev(LANES, 0), cur(LANES, 0), cur(LANES, 0),
                  cst(1, gw), cst(1, KV), cst(gw, gw)],
        out_specs=cur(SW, 0),
    )
    return pl.pallas_call(
        _swa_kernel,
        grid_spec=grid_spec,
        out_shape=jax.ShapeDtypeStruct((T, SW), bf16),
        compiler_params=_cparams(("parallel", "parallel")),
        name="swa",
    )(sinks.astype(f32), proj, proj, proj, proj, proj, cos_s, sin_s, cos_s, sin_s, qg, kg, gmat)


def _out_proj_kernel(x_ref, a_ref, b_ref, wa_ref, wb_ref, o_ref):
    acc = jnp.dot(a_ref[...], wa_ref[...], preferred_element_type=f32)
    acc = acc + jnp.dot(b_ref[...], wb_ref[...], preferred_element_type=f32)
    o_ref[...] = x_ref[...] + acc


def _out_proj(x2d, ret, swa, w):
    T, D = x2d.shape
    R, SW = ret.shape[1], swa.shape[1]
    assert R == SW
    tm = _pick(T, (1024, 512, 256))
    tn = _pick(D, (512, 256, 128))
    return pl.pallas_call(
        _out_proj_kernel,
        grid=(T // tm, D // tn),
        in_specs=[pl.BlockSpec((tm, tn), lambda i, j: (i, j)),
                  pl.BlockSpec((tm, R), lambda i, j: (i, 0)),
                  pl.BlockSpec((tm, SW), lambda i, j: (i, 0)),
                  pl.BlockSpec((R, tn), lambda i, j: (0, j)),
                  pl.BlockSpec((SW, tn), lambda i, j: (1, j))],
        out_specs=pl.BlockSpec((tm, tn), lambda i, j: (i, j)),
        out_shape=jax.ShapeDtypeStruct((T, D), f32),
        compiler_params=_cparams(("parallel", "arbitrary")),
        name="out_proj",
    )(x2d, ret, swa, w, w)


def _router_kernel(h_ref, g_ref, wcat_ref, whi_ref, br_ref, hp_ref, idx_ref, gate_ref):
    h = h_ref[...]
    hn = h * lax.rsqrt(jnp.mean(h * h, axis=-1, keepdims=True) + NORM_EPS) * g_ref[...]
    E = whi_ref.shape[1]
    hn_hi = hn.astype(bf16)
    hn_hi32 = hn_hi.astype(f32)
    hn_lo = (hn - hn_hi32).astype(bf16)
    l1 = jnp.dot(hn_hi, wcat_ref[...], preferred_element_type=f32)
    l2 = jnp.dot(hn_lo, whi_ref[...], preferred_element_type=f32)
    logits = l1[:, :E] + l1[:, E:] + l2 + br_ref[...]
    lane = lax.broadcasted_iota(jnp.int32, logits.shape, 1)
    vals, idxs = [], []
    cur = logits
    for _ in range(TOP_K):
        m = jnp.max(cur, axis=-1, keepdims=True)
        i = jnp.min(jnp.where(cur == m, lane, E), axis=-1, keepdims=True)
        vals.append(m)
        idxs.append(i)
        cur = jnp.where(lane == i, -jnp.inf, cur)
    ex = [jnp.exp(v - vals[0]) for v in vals]
    tot = ex[0]
    for e in ex[1:]:
        tot = tot + e
    col = lax.broadcasted_iota(jnp.int32, idx_ref.shape, 1)
    idx_out = jnp.zeros(idx_ref.shape, jnp.int32)
    gate_out = jnp.zeros(gate_ref.shape, f32)
    for k in range(TOP_K):
        idx_out = jnp.where(col == k, idxs[k], idx_out)
        gate_out = jnp.where(col == k, ex[k] / tot, gate_out)
    idx_ref[...] = idx_out
    gate_ref[...] = gate_out
    half = hn.shape[1] // 2
    hp_ref[...] = _pack_bf16_pair(hn[:, :half], hn[:, half:])


def _router(h2d, gain, w_router, b_router):
    T, D = h2d.shape
    E = w_router.shape[1]
    w_hi = w_router.astype(bf16)
    w_lo = (w_router - w_hi.astype(f32)).astype(bf16)
    wcat = jnp.concatenate([w_hi, w_lo], axis=1)
    tr = min(256, T)
    row = lambda w: pl.BlockSpec((tr, w), lambda i: (i, 0))
    cst = lambda r, w: pl.BlockSpec((r, w), lambda i: (0, 0))
    return pl.pallas_call(
        _router_kernel,
        grid=(T // tr,),
        in_specs=[row(D), cst(1, D), cst(D, 2 * E), cst(D, E), cst(1, E)],
        out_specs=[row(D // 2), row(TOP_K), row(TOP_K)],
        out_shape=[jax.ShapeDtypeStruct((T, D // 2), u32),
                   jax.ShapeDtypeStruct((T, TOP_K), jnp.int32),
                   jax.ShapeDtypeStruct((T, TOP_K), f32)],
        compiler_params=_cparams(("parallel",)),
        name="ffn_norm_router",
    )(h2d, gain[None, :], wcat, w_hi, b_router[None, :])


def _moe_kernel(te_ref, tr_ref, rowtok_hbm, hp_hbm, wg_ref, wl_ref, woa_ref, wob_ref,
                bg_ref, bl_ref, boa_ref, bob_ref, o_ref,
                xg_ref, act_ref, wbuf_ref, idx_ref, idx_sem, row_sem, *, n_steps_in, chunk):
    i = pl.program_id(0)
    j = pl.program_id(1)
    n_tiles = pl.num_programs(0)
    tm = act_ref.shape[0]
    half = xg_ref.shape[2]
    F = act_ref.shape[1]
    tn1 = wg_ref.shape[2]
    tn2 = woa_ref.shape[2]
    n_chunks = tm // chunk
    cg = chunk // SUBLANES
    rows = tr_ref[i]

    def idx_copy(t):
        return pltpu.make_async_copy(rowtok_hbm.at[t], idx_ref, idx_sem)

    def gather_start(n_rows):
        for c in range(n_chunks):
            @pl.when(n_rows > c * chunk)
            def _():
                def body(g, carry):
                    for u in range(SUBLANES):
                        tok = idx_ref[c * chunk + g * SUBLANES + u]
                        pltpu.make_async_copy(hp_hbm.at[tok >> 3, tok & 7], xg_ref.at[c * cg + g, u], row_sem).start()
                    return carry
                lax.fori_loop(0, cg, body, 0)

    def gather_wait(n_rows):
        for c in range(n_chunks):
            @pl.when(n_rows > c * chunk)
            def _():
                pltpu.make_async_copy(hp_hbm.at[pl.ds(0, cg)], xg_ref.at[pl.ds(c * cg, cg)], row_sem).wait()

    @pl.when((j == 0) & (rows > 0))
    def _():
        @pl.when(i == 0)
        def _():
            idx_copy(0).start()
            idx_copy(0).wait()
            gather_start(rows)

        gather_wait(rows)

    nxt = jnp.minimum(i + 1, n_tiles - 1)
    rows_next = jnp.where(i + 1 < n_tiles, tr_ref[nxt], 0)

    @pl.when((j == n_steps_in - 1) & (rows_next > 0))
    def _():
        idx_copy(nxt).start()

    @pl.when((j < n_steps_in) & (rows > 0))
    def _():
        wbuf_ref[:, :tn1] = wg_ref[0].astype(bf16)
        wbuf_ref[:, tn1:] = wl_ref[0].astype(bf16)
        for c in range(n_chunks):
            @pl.when(rows > c * chunk)
            def _():
                w = xg_ref[c * cg:(c + 1) * cg].reshape(chunk, half)
                x = jnp.concatenate([_unpack_lo(w).astype(bf16), _unpack_hi(w).astype(bf16)], axis=1)
                hm = jnp.dot(x, wbuf_ref[...], preferred_element_type=f32)
                hg = jnp.minimum(hm[:, :tn1] + bg_ref[0], SWIGLU_LIMIT)
                hl = jnp.clip(hm[:, tn1:] + bl_ref[0], -SWIGLU_LIMIT, SWIGLU_LIMIT)
                a = (hg * jax.nn.sigmoid(SWIGLU_ALPHA * hg) * (hl + 1.0)).astype(bf16)
                for jj in range(n_steps_in):
                    @pl.when(j == jj)
                    def _():
                        act_ref[c * chunk:(c + 1) * chunk, jj * tn1:(jj + 1) * tn1] = a

    @pl.when((j == n_steps_in) & (rows_next > 0))
    def _():
        idx_copy(nxt).wait()
        gather_start(rows_next)

    @pl.when((j >= n_steps_in) & (rows > 0))
    def _():
        wbuf_ref[:F, :tn2] = woa_ref[0].astype(bf16)
        wbuf_ref[:F, tn2:] = wob_ref[0].astype(bf16)
        for c in range(n_chunks):
            rs = slice(c * chunk, (c + 1) * chunk)

            @pl.when(rows > c * chunk)
            def _():
                o = jnp.dot(act_ref[rs, :], wbuf_ref[:F, :], preferred_element_type=f32)
                o_ref[rs, :] = _pack_bf16_pair(o[:, :tn2] + boa_ref[0], o[:, tn2:] + bob_ref[0])

            @pl.when(rows <= c * chunk)
            def _():
                o_ref[rs, :] = jnp.zeros((chunk, tn2), u32)

    @pl.when((j >= n_steps_in) & (rows == 0))
    def _():
        o_ref[...] = jnp.zeros_like(o_ref)


def _moe(hp, tile_expert, tile_rows, row_tok, w_in, b_in, w_out, b_out):
    T, half = hp.shape
    D = 2 * half
    E, F = w_out.shape[0], w_out.shape[1]
    n_tiles, tm = row_tok.shape
    chunk = min(MOE_CHUNK, tm)
    tn1 = _pick(F, (256, 128))
    tn2 = _pick(half, (256, 128))
    assert tn1 == tn2 and F <= D
    J1, J2 = F // tn1, half // tn2

    def im(f):
        return lambda i, j, te, tr: f(i, j, te)

    def c1(j):
        return jnp.minimum(j, J1 - 1)

    def c2(j):
        return jnp.maximum(j - J1, 0)

    grid_spec = pltpu.PrefetchScalarGridSpec(
        num_scalar_prefetch=2,
        grid=(n_tiles, J1 + J2),
        in_specs=[
            pl.BlockSpec(memory_space=pl.ANY),
            pl.BlockSpec(memory_space=pl.ANY),
            pl.BlockSpec((1, D, tn1), im(lambda i, j, te: (te[i], 0, c1(j)))),
            pl.BlockSpec((1, D, tn1), im(lambda i, j, te: (te[i], 0, J1 + c1(j)))),
            pl.BlockSpec((1, F, tn2), im(lambda i, j, te: (te[i], 0, c2(j)))),
            pl.BlockSpec((1, F, tn2), im(lambda i, j, te: (te[i], 0, J2 + c2(j)))),
            pl.BlockSpec((1, 1, tn1), im(lambda i, j, te: (te[i], 0, c1(j)))),
            pl.BlockSpec((1, 1, tn1), im(lambda i, j, te: (te[i], 0, J1 + c1(j)))),
            pl.BlockSpec((1, 1, tn2), im(lambda i, j, te: (te[i], 0, c2(j)))),
            pl.BlockSpec((1, 1, tn2), im(lambda i, j, te: (te[i], 0, J2 + c2(j)))),
        ],
        out_specs=pl.BlockSpec((tm, tn2), im(lambda i, j, te: (i, c2(j)))),
        scratch_shapes=[
            pltpu.VMEM((tm // SUBLANES, SUBLANES, half), u32),
            pltpu.VMEM((tm, F), bf16),
            pltpu.VMEM((D, 2 * tn1), bf16),
            pltpu.SMEM((tm,), jnp.int32),
            pltpu.SemaphoreType.DMA,
            pltpu.SemaphoreType.DMA,
        ],
    )
    b_in3, b_out3 = b_in[:, None, :], b_out[:, None, :]
    return pl.pallas_call(
        functools.partial(_moe_kernel, n_steps_in=J1, chunk=chunk),
        grid_spec=grid_spec,
        out_shape=jax.ShapeDtypeStruct((n_tiles * tm, half), u32),
        compiler_params=_cparams(("arbitrary", "arbitrary")),
        name="moe_experts",
    )(tile_expert, tile_rows, row_tok, hp.reshape(T // SUBLANES, SUBLANES, half),
      w_in, w_in, w_out, w_out, b_in3, b_in3, b_out3, b_out3)


def _combine_kernel(pos_hbm, o_hbm, h_ref, gate_ref, out_ref, buf_ref, idx0_ref, idx1_ref, idx_sem, row_sem):
    i = pl.program_id(0)
    n = pl.num_programs(0)
    tq, D = h_ref.shape
    half = D // 2
    qg = tq // SUBLANES
    idx_refs = (idx0_ref, idx1_ref)

    def idx_copy(t, s):
        return pltpu.make_async_copy(pos_hbm.at[t], idx_refs[s], idx_sem.at[s])

    def gather_start(s):
        def body(g, carry):
            for u in range(SUBLANES):
                for k in range(TOP_K):
                    r = idx_refs[s][(g * SUBLANES + u) * TOP_K + k]
                    pltpu.make_async_copy(o_hbm.at[r >> 3, r & 7], buf_ref.at[s, k * qg + g, u], row_sem.at[s]).start()
            return carry
        lax.fori_loop(0, qg, body, 0)

    @pl.when(i == 0)
    def _():
        idx_copy(0, 0).start()
        idx_copy(0, 0).wait()
        gather_start(0)

        @pl.when(n > 1)
        def _():
            idx_copy(1, 1).start()

    for s in range(2):
        @pl.when(i % 2 == s)
        def _():
            @pl.when(i + 1 < n)
            def _():
                idx_copy(i + 1, 1 - s).wait()
                gather_start(1 - s)

            @pl.when(i + 2 < n)
            def _():
                idx_copy(i + 2, s).start()

            pltpu.make_async_copy(o_hbm.at[pl.ds(0, TOP_K * qg)], buf_ref.at[s], row_sem.at[s]).wait()
            gate = gate_ref[...]
            acc_lo = h_ref[:, :half]
            acc_hi = h_ref[:, half:]
            for k in range(TOP_K):
                w = buf_ref[s, k * qg:(k + 1) * qg].reshape(tq, half)
                gk = gate[:, k:k + 1]
                acc_lo = acc_lo + gk * _unpack_lo(w)
                acc_hi = acc_hi + gk * _unpack_hi(w)
            out_ref[:, :half] = acc_lo
            out_ref[:, half:] = acc_hi


def _combine(h2d, gates, o_packed, pos):
    T, D = h2d.shape
    half = D // 2
    P = o_packed.shape[0]
    tq = min(128, T)
    pos2 = pos.reshape(T // tq, tq * TOP_K)
    return pl.pallas_call(
        _combine_kernel,
        grid=(T // tq,),
        in_specs=[pl.BlockSpec(memory_space=pl.ANY), pl.BlockSpec(memory_space=pl.ANY),
                  pl.BlockSpec((tq, D), lambda i: (i, 0)), pl.BlockSpec((tq, TOP_K), lambda i: (i, 0))],
        out_specs=pl.BlockSpec((tq, D), lambda i: (i, 0)),
        out_shape=jax.ShapeDtypeStruct((T, D), f32),
        scratch_shapes=[pltpu.VMEM((2, TOP_K * tq // SUBLANES, SUBLANES, half), u32),
                        pltpu.SMEM((TOP_K * tq,), jnp.int32),
                        pltpu.SMEM((TOP_K * tq,), jnp.int32),
                        pltpu.SemaphoreType.DMA((2,)),
                        pltpu.SemaphoreType.DMA((2,))],
        compiler_params=_cparams(("arbitrary",)),
        name="moe_combine",
    )(pos2, o_packed.reshape(P // SUBLANES, SUBLANES, half), h2d, gates)


def _routing_tables(top_idx, n_experts, tm):
    T = top_idx.shape[0]
    M = T * TOP_K
    e_flat = top_idx.reshape(-1)
    ids = jnp.arange(M, dtype=jnp.int32)
    _, order = lax.sort((e_flat, ids), num_keys=1, is_stable=True)
    _, inv = lax.sort((order, ids), num_keys=1)
    experts = jnp.arange(n_experts, dtype=jnp.int32)
    counts = jnp.sum((e_flat[:, None] == experts[None, :]).astype(jnp.int32), axis=0)
    starts = jnp.cumsum(counts) - counts
    tiles_per = (counts + tm - 1) // tm
    tile_end = jnp.cumsum(tiles_per)
    tile_start = tile_end - tiles_per
    rank = inv - starts[e_flat]
    pos = (tile_start[e_flat] + rank // tm) * tm + rank % tm
    n_tiles = M // tm + n_experts
    t_ids = jnp.arange(n_tiles, dtype=jnp.int32)
    te = jnp.minimum(jnp.sum((tile_end[None, :] <= t_ids[:, None]).astype(jnp.int32), axis=1), n_experts - 1)
    n_used = tile_end[-1]
    used = t_ids < n_used
    tile_expert = jnp.where(used, te, te[jnp.maximum(n_used - 1, 0)])
    k_in = t_ids - tile_start[te]
    tile_rows = jnp.where(used, jnp.clip(counts[te] - k_in * tm, 0, tm), 0).astype(jnp.int32)
    src = jnp.where(used, starts[te] + k_in * tm, 0)
    tok_sorted = jnp.concatenate([order // TOP_K, jnp.zeros((tm,), jnp.int32)])
    row_tok = jax.vmap(lambda s: lax.dynamic_slice(tok_sorted, (s,), (tm,)))(src)
    return tile_expert.astype(jnp.int32), tile_rows, row_tok, pos.reshape(T, TOP_K)


def kernel(x, positions, attn_norm_gain, w_in_proj, swa_q_gain, swa_k_gain, swa_sinks, w_out_proj, ffn_norm_gain,
           w_router, b_router, w_expert_in, b_expert_in, w_expert_out, b_expert_out):
    B, S, D = x.shape
    T = B * S
    depth = w_in_proj.shape[0]
    R = D // 2
    SW = D - R
    n_ret = R // RET_HEAD_DIM
    n_swa = swa_sinks.shape[1]
    KV = (n_swa // SWA_GROUP) * SWA_HEAD_DIM
    n_experts = w_router.shape[2]
    assert w_in_proj.shape[2] == 4 * R + SW + 2 * KV and SW == n_swa * SWA_HEAD_DIM and S % WINDOW == 0

    cos_r, sin_r, cos_s, sin_s = _rope_tables(positions.reshape(T, 1))
    h = x.reshape(T, D)
    for layer in range(depth):
        xn = _rmsnorm_bf16(h, attn_norm_gain[layer])
        proj = _in_proj(xn, w_in_proj[layer].astype(bf16))
        ret = _retention(proj, cos_r, sin_r, B, S, n_ret)
        swa = _swa(proj, cos_s, sin_s, swa_sinks[layer], swa_q_gain[layer], swa_k_gain[layer], B, S, 4 * R, n_swa)
        h = _out_proj(h, ret, swa, w_out_proj[layer].astype(bf16))

        hp, top_idx, gates = _router(h, ffn_norm_gain[layer], w_router[layer], b_router[layer])
        tm = min(MOE_TILE, T)
        tile_expert, tile_rows, row_tok, pos = _routing_tables(top_idx, n_experts, tm)
        o_packed = _moe(hp, tile_expert, tile_rows, row_tok,
                        w_expert_in[layer], b_expert_in[layer], w_expert_out[layer], b_expert_out[layer])
        h = _combine(h, gates, o_packed, pos)
    return h.reshape(B, S, D)
```

```python
import functools

import jax
import jax.numpy as jnp
from jax import lax
from jax.experimental import pallas as pl
from jax.experimental.pallas import tpu as pltpu

RET_HEAD_DIM = 256
SWA_HEAD_DIM = 64
SWA_GROUP = 8
WINDOW = 128
ROPE_THETA = 10000.0
TOP_K = 4
SWIGLU_LIMIT = 7.0
SWIGLU_ALPHA = 1.702
NORM_EPS = 1e-5
NEG_INF = -1e30

RET_CHUNK = 256
MOE_TILE_TARGET = 1024
MOE_TILE_SLACK = 1.0625
LANES = 128
SUBLANES = 8
VMEM_LIMIT = 56 * 1024 * 1024

f32 = jnp.float32
bf16 = jnp.bfloat16
u32 = jnp.uint32


def _cparams(sem, vmem=VMEM_LIMIT):
    return pltpu.CompilerParams(dimension_semantics=sem, vmem_limit_bytes=vmem)


def _pick(n, prefs):
    for p in prefs:
        if n % p == 0:
            return p
    return n


def _pack_bf16_pair(lo, hi):
    lo_b = lax.bitcast_convert_type(lo.astype(bf16).astype(f32), u32)
    hi_b = lax.bitcast_convert_type(hi.astype(bf16).astype(f32), u32)
    return hi_b | (lo_b >> 16)


def _unpack_lo(w):
    return lax.bitcast_convert_type(w << 16, f32)


def _unpack_hi(w):
    return lax.bitcast_convert_type(w & jnp.uint32(0xFFFF0000), f32)


def _rope_kernel(pos_ref, fr_ref, fs_ref, ss_ref, cr_ref, snr_ref, cs_ref, sns_ref):
    pos = pos_ref[...].astype(f32)
    ang_r = pos * fr_ref[...]
    cr_ref[...] = jnp.cos(ang_r)
    snr_ref[...] = jnp.sin(ang_r)
    ang_s = pos * fs_ref[...]
    cs_ref[...] = jnp.cos(ang_s)
    sns_ref[...] = jnp.sin(ang_s) * ss_ref[...]


def _rope_tables(pos_flat):
    T = pos_flat.shape[0]
    half = RET_HEAD_DIM // 2
    assert half == LANES
    fr = (1.0 / (ROPE_THETA ** jnp.linspace(0.0, 1.0, half, dtype=f32)))[None, :]
    d = SWA_HEAD_DIM
    inv_s = ROPE_THETA ** (-jnp.arange(0, d, 2, dtype=f32) / d)
    fs = jnp.tile(inv_s, LANES // (d // 2))[None, :]
    sgn = jnp.concatenate([-jnp.ones((d // 2,), f32), jnp.ones((d // 2,), f32)])
    ss = jnp.tile(sgn, LANES // d)[None, :]
    tr = min(512, T)
    row = lambda w: pl.BlockSpec((tr, w), lambda i: (i, 0))
    cst = pl.BlockSpec((1, LANES), lambda i: (0, 0))
    return pl.pallas_call(
        _rope_kernel,
        grid=(T // tr,),
        in_specs=[row(1), cst, cst, cst],
        out_specs=[row(LANES)] * 4,
        out_shape=[jax.ShapeDtypeStruct((T, LANES), f32)] * 4,
        compiler_params=_cparams(("parallel",)),
        name="rope_tables",
    )(pos_flat, fr, fs, ss)


def _rmsnorm_kernel(x_ref, g_ref, o_ref):
    x = x_ref[...]
    y = x * lax.rsqrt(jnp.mean(x * x, axis=-1, keepdims=True) + NORM_EPS)
    o_ref[...] = (y * g_ref[...]).astype(o_ref.dtype)


def _rmsnorm_bf16(x2d, gain):
    T, D = x2d.shape
    tr = min(256, T)
    return pl.pallas_call(
        _rmsnorm_kernel,
        grid=(T // tr,),
        in_specs=[pl.BlockSpec((tr, D), lambda i: (i, 0)), pl.BlockSpec((1, D), lambda i: (0, 0))],
        out_specs=pl.BlockSpec((tr, D), lambda i: (i, 0)),
        out_shape=jax.ShapeDtypeStruct((T, D), bf16),
        compiler_params=_cparams(("parallel",)),
        name="attn_rmsnorm",
    )(x2d, gain[None, :])


def _matmul_kernel(x_ref, w_ref, o_ref):
    o_ref[...] = jnp.dot(x_ref[...], w_ref[...], preferred_element_type=f32).astype(o_ref.dtype)


def _in_proj(xn, w):
    T, D = xn.shape
    N = w.shape[1]
    tm = _pick(T, (1024, 512, 256))
    tn = _pick(N, (768, 512, 256, 128))
    return pl.pallas_call(
        _matmul_kernel,
        grid=(T // tm, N // tn),
        in_specs=[pl.BlockSpec((tm, D), lambda i, j: (i, 0)), pl.BlockSpec((D, tn), lambda i, j: (0, j))],
        out_specs=pl.BlockSpec((tm, tn), lambda i, j: (i, j)),
        out_shape=jax.ShapeDtypeStruct((T, N), bf16),
        compiler_params=_cparams(("parallel", "arbitrary")),
        name="in_proj",
    )(xn, w)


def _retention_kernel(q_ref, k_ref, v_ref, g_ref, cos_ref, sin_ref, dec_ref, xi_ref, zeta_ref, cd_ref,
                      o_ref, state_ref):
    n = pl.program_id(1)
    hd = RET_HEAD_DIM
    n_heads = q_ref.shape[1] // hd

    @pl.when(n == 0)
    def _():
        state_ref[...] = jnp.zeros_like(state_ref)

    c = cos_ref[...]
    s = sin_ref[...]
    cos = jnp.concatenate([c, c], axis=1)
    sin = jnp.concatenate([-s, s], axis=1)

    def rot(x):
        return x * cos + pltpu.roll(x, hd // 2, 1) * sin

    for h in range(n_heads):
        sl = slice(h * hd, (h + 1) * hd)
        qr = rot(q_ref[:, sl].astype(f32))
        kr = rot(k_ref[:, sl].astype(f32)) * (hd ** -0.5)
        v = v_ref[:, sl]
        scores = lax.dot_general(qr.astype(bf16), kr.astype(bf16), (((1,), (1,)), ((), ())),
                                 preferred_element_type=f32)
        scores = scores * dec_ref[h]
        intra = jnp.dot(scores.astype(bf16), v, preferred_element_type=f32)
        state = state_ref[h]
        cross = jnp.dot((qr * xi_ref[h]).astype(bf16), state.astype(bf16), preferred_element_type=f32)
        kz = (kr * zeta_ref[h]).astype(bf16)
        kv = lax.dot_general(kz, v, (((0,), (0,)), ((), ())), preferred_element_type=f32)
        state_ref[h] = state * cd_ref[h] + kv
        o = intra + cross
        o = o * lax.rsqrt(jnp.mean(o * o, axis=-1, keepdims=True) + NORM_EPS)
        g = g_ref[:, sl].astype(f32)
        o_ref[:, sl] = (o * (g * jax.nn.sigmoid(g))).astype(o_ref.dtype)


def _retention(proj, cos_r, sin_r, B, S, n_heads):
    T = proj.shape[0]
    C = min(RET_CHUNK, S)
    NC = S // C
    hd = RET_HEAD_DIM
    R = n_heads * hd
    log_gamma = jnp.log(1.0 - 2.0 ** (-5.0 - jnp.arange(n_heads, dtype=f32)))
    idx = jnp.arange(C, dtype=f32)
    rel = idx[:, None] - idx[None, :]
    decay = jnp.where(rel >= 0, jnp.exp(log_gamma[:, None, None] * jnp.maximum(rel, 0.0)), 0.0)
    xi = jnp.exp(log_gamma[:, None] * (idx + 1.0))[:, :, None]
    zeta = jnp.exp(log_gamma[:, None] * (C - 1.0 - idx))[:, :, None]
    cdec = jnp.exp(log_gamma * C)[:, None, None]

    col = lambda j: pl.BlockSpec((C, R), lambda b, n: (b * NC + n, j))
    tab = pl.BlockSpec((C, LANES), lambda b, n: (b * NC + n, 0))
    per_head = lambda s: pl.BlockSpec((n_heads,) + s, lambda b, n: (0, 0, 0))
    return pl.pallas_call(
        _retention_kernel,
        grid=(B, NC),
        in_specs=[col(0), col(1), col(2), col(3), tab, tab,
                  per_head((C, C)), per_head((C, 1)), per_head((C, 1)), per_head((1, 1))],
        out_specs=pl.BlockSpec((C, R), lambda b, n: (b * NC + n, 0)),
        out_shape=jax.ShapeDtypeStruct((T, R), bf16),
        scratch_shapes=[pltpu.VMEM((n_heads, hd, hd), f32)],
        compiler_params=_cparams(("parallel", "arbitrary")),
        name="retention",
    )(proj, proj, proj, proj, cos_r, sin_r, decay, xi, zeta, cdec)


def _swap32(x):
    w = x.shape[-1]
    lane = lax.broadcasted_iota(jnp.int32, x.shape, 1)
    first = (lane % SWA_HEAD_DIM) < (SWA_HEAD_DIM // 2)
    return jnp.where(first, pltpu.roll(x, w - SWA_HEAD_DIM // 2, 1), pltpu.roll(x, SWA_HEAD_DIM // 2, 1))


def _swa_kernel(sink_ref, q_ref, kp_ref, kc_ref, vp_ref, vc_ref, cosp_ref, sinp_ref, cosc_ref, sinc_ref,
                qg_ref, kg_ref, gmat_ref, o_ref):
    n = pl.program_id(1)
    W = WINDOW
    d = SWA_HEAD_DIM
    gw = SWA_GROUP * d
    kvw = kc_ref.shape[1]
    n_kv = kvw // d
    pairs = gw // LANES

    lane = lax.broadcasted_iota(jnp.int32, (W, LANES), 1)
    is_lo = lane < d
    cosc = cosc_ref[...]
    sinc = sinc_ref[...]

    def rms_rot(x, gain, gsub, cos, sin):
        reps = x.shape[1] // LANES
        ssq = jnp.dot((x * x).astype(bf16), gsub, preferred_element_type=f32)
        xn = x * lax.rsqrt(ssq * (1.0 / d) + NORM_EPS) * gain
        return xn * jnp.concatenate([cos] * reps, axis=1) + _swap32(xn) * jnp.concatenate([sin] * reps, axis=1)

    gk = gmat_ref[:kvw, :kvw]
    krp = rms_rot(kp_ref[...].astype(f32), kg_ref[...], gk, cosp_ref[...], sinp_ref[...])
    krc = rms_rot(kc_ref[...].astype(f32), kg_ref[...], gk, cosc, sinc)
    vp = vp_ref[...].astype(f32)
    vc = vc_ref[...].astype(f32)

    neg_prev = jnp.where(n > 0, 0.0, NEG_INF)
    row = lax.broadcasted_iota(jnp.int32, (2 * W, W), 0) % W
    colj = lax.broadcasted_iota(jnp.int32, (2 * W, W), 1)
    tri = colj <= row
    top = lax.broadcasted_iota(jnp.int32, (2 * W, 1), 0) < W

    for g in range(n_kv):
        c, hi_half = divmod(g, LANES // d)

        def halves(x):
            xc = x[:, c * LANES:(c + 1) * LANES]
            r = pltpu.roll(xc, d, 1)
            return (r, xc) if hi_half else (xc, r)

        def keys(x):
            a, b = halves(x)
            return jnp.where(is_lo, a, b)

        def vals(x):
            a, b = halves(x)
            v_lo = jnp.where(is_lo, a, jnp.where(lane == d, 1.0, 0.0))
            v_hi = jnp.where(is_lo, jnp.where(lane == 0, 1.0, 0.0), b)
            return v_lo, v_hi

        kk = jnp.concatenate([keys(krp), keys(krc)], axis=0).astype(bf16)
        vlo_p, vhi_p = vals(vp)
        vlo_c, vhi_c = vals(vc)
        v_lo = jnp.concatenate([vlo_p, vlo_c], axis=0).astype(bf16)
        v_hi = jnp.concatenate([vhi_p, vhi_c], axis=0).astype(bf16)

        q = q_ref[:, g * gw:(g + 1) * gw].astype(f32)
        qr = rms_rot(q, qg_ref[...], gmat_ref[...], cosc, sinc) * (d ** -0.5)

        for p in range(pairs):
            qp = qr[:, p * LANES:(p + 1) * LANES]
            lhs = jnp.concatenate([jnp.where(is_lo, qp, 0.0), jnp.where(is_lo, 0.0, qp)], axis=0).astype(bf16)
            lg = lax.dot_general(lhs, kk, (((1,), (1,)), ((), ())), preferred_element_type=f32)
            merged = jnp.where(tri, lg[:, W:], lg[:, :W] + neg_prev)
            h0 = g * SWA_GROUP + 2 * p
            sink = jnp.where(top, sink_ref[h0], sink_ref[h0 + 1])
            m = jnp.maximum(jnp.max(merged, axis=-1, keepdims=True), sink)
            eb = jnp.exp(merged - m).astype(bf16)
            zero = jnp.zeros_like(eb)
            probs = jnp.concatenate([jnp.where(tri, zero, eb), jnp.where(tri, eb, zero)], axis=1)
            o_lo = jnp.dot(probs[:W], v_lo, preferred_element_type=f32)
            o_hi = jnp.dot(probs[W:], v_hi, preferred_element_type=f32)
            esink = jnp.exp(sink - m)
            r_lo = 1.0 / (o_lo[:, d:d + 1] + esink[:W])
            r_hi = 1.0 / (o_hi[:, 0:1] + esink[W:])
            out = jnp.where(is_lo, o_lo * r_lo, o_hi * r_hi)
            o_ref[:, g * gw + p * LANES:g * gw + (p + 1) * LANES] = out.astype(o_ref.dtype)


def _swa(proj, cos_s, sin_s, sinks, q_gain, k_gain, B, S, q_off, n_heads):
    T = proj.shape[0]
    W = WINDOW
    NB = S // W
    d = SWA_HEAD_DIM
    n_kv = n_heads // SWA_GROUP
    SW, KV = n_heads * d, n_kv * d
    gw = SWA_GROUP * d
    assert q_off % SW == 0 and (q_off + SW) % KV == 0 and KV % LANES == 0 and KV <= gw
    qg = jnp.tile(q_gain.astype(f32), SWA_GROUP)[None, :]
    kg = jnp.tile(k_gain.astype(f32), n_kv)[None, :]
    head_id = jnp.arange(gw) // d
    gmat = (head_id[:, None] == head_id[None, :]).astype(bf16)
    qb = q_off // SW
    kb = (q_off + SW) // KV

    def im(f):
        return lambda b, n, s: f(b, n)

    cur = lambda w, j: pl.BlockSpec((W, w), im(lambda b, n: (b * NB + n, j)))
    prev = lambda w, j: pl.BlockSpec((W, w), im(lambda b, n: (b * NB + jnp.maximum(n - 1, 0), j)))
    cst = lambda r, w: pl.BlockSpec((r, w), im(lambda b, n: (0, 0)))
    grid_spec = pltpu.PrefetchScalarGridSpec(
        num_scalar_prefetch=1,
        grid=(B, NB),
        in_specs=[cur(SW, qb), prev(KV, kb), cur(KV, kb), prev(KV, kb + 1), cur(KV, kb + 1),
                  prev(LANES, 0), prev(LANES, 0), cur(LANES, 0), cur(LANES, 0),
                  cst(1, gw), cst(1, KV), cst(gw, gw)],
        out_specs=cur(SW, 0),
    )
    return pl.pallas_call(
        _swa_kernel,
        grid_spec=grid_spec,
        out_shape=jax.ShapeDtypeStruct((T, SW), bf16),
        compiler_params=_cparams(("parallel", "parallel")),
        name="swa",
    )(sinks.astype(f32), proj, proj, proj, proj, proj, cos_s, sin_s, cos_s, sin_s, qg, kg, gmat)


def _out_proj_kernel(x_ref, a_ref, b_ref, wa_ref, wb_ref, o_ref):
    acc = jnp.dot(a_ref[...], wa_ref[...], preferred_element_type=f32)
    acc = acc + jnp.dot(b_ref[...], wb_ref[...], preferred_element_type=f32)
    o_ref[...] = x_ref[...] + acc


def _out_proj(x2d, ret, swa, w):
    T, D = x2d.shape
    R, SW = ret.shape[1], swa.shape[1]
    assert R == SW
    tm = _pick(T, (1024, 512, 256))
    tn = _pick(D, (512, 256, 128))
    return pl.pallas_call(
        _out_proj_kernel,
        grid=(T // tm, D // tn),
        in_specs=[pl.BlockSpec((tm, tn), lambda i, j: (i, j)),
                  pl.BlockSpec((tm, R), lambda i, j: (i, 0)),
                  pl.BlockSpec((tm, SW), lambda i, j: (i, 0)),
                  pl.BlockSpec((R, tn), lambda i, j: (0, j)),
                  pl.BlockSpec((SW, tn), lambda i, j: (1, j))],
        out_specs=pl.BlockSpec((tm, tn), lambda i, j: (i, j)),
        out_shape=jax.ShapeDtypeStruct((T, D), f32),
        compiler_params=_cparams(("parallel", "arbitrary")),
        name="out_proj",
    )(x2d, ret, swa, w, w)


def _router_kernel(h_ref, g_ref, wcat_ref, whi_ref, br_ref, hp_ref, idx_ref, gate_ref):
    h = h_ref[...]
    hn = h * lax.rsqrt(jnp.mean(h * h, axis=-1, keepdims=True) + NORM_EPS) * g_ref[...]
    E = whi_ref.shape[1]
    hn_hi = hn.astype(bf16)
    hn_hi32 = hn_hi.astype(f32)
    hn_lo = (hn - hn_hi32).astype(bf16)
    l1 = jnp.dot(hn_hi, wcat_ref[...], preferred_element_type=f32)
    l2 = jnp.dot(hn_lo, whi_ref[...], preferred_element_type=f32)
    logits = l1[:, :E] + l1[:, E:] + l2 + br_ref[...]
    lane = lax.broadcasted_iota(jnp.int32, logits.shape, 1)
    vals, idxs = [], []
    cur = logits
    for _ in range(TOP_K):
        m = jnp.max(cur, axis=-1, keepdims=True)
        i = jnp.min(jnp.where(cur == m, lane, E), axis=-1, keepdims=True)
        vals.append(m)
        idxs.append(i)
        cur = jnp.where(lane == i, -jnp.inf, cur)
    ex = [jnp.exp(v - vals[0]) for v in vals]
    tot = ex[0]
    for e in ex[1:]:
        tot = tot + e
    col = lax.broadcasted_iota(jnp.int32, idx_ref.shape, 1)
    idx_out = jnp.zeros(idx_ref.shape, jnp.int32)
    gate_out = jnp.zeros(gate_ref.shape, f32)
    for k in range(TOP_K):
        idx_out = jnp.where(col == k, idxs[k], idx_out)
        gate_out = jnp.where(col == k, ex[k] / tot, gate_out)
    idx_ref[...] = idx_out
    gate_ref[...] = gate_out
    half = hn.shape[1] // 2
    hp_ref[...] = _pack_bf16_pair(hn[:, :half], hn[:, half:])


def _router(h2d, gain, w_router, b_router):
    T, D = h2d.shape
    E = w_router.shape[1]
    w_hi = w_router.astype(bf16)
    w_lo = (w_router - w_hi.astype(f32)).astype(bf16)
    wcat = jnp.concatenate([w_hi, w_lo], axis=1)
    tr = min(256, T)
    row = lambda w: pl.BlockSpec((tr, w), lambda i: (i, 0))
    cst = lambda r, w: pl.BlockSpec((r, w), lambda i: (0, 0))
    return pl.pallas_call(
        _router_kernel,
        grid=(T // tr,),
        in_specs=[row(D), cst(1, D), cst(D, 2 * E), cst(D, E), cst(1, E)],
        out_specs=[row(D // 2), row(TOP_K), row(TOP_K)],
        out_shape=[jax.ShapeDtypeStruct((T, D // 2), u32),
                   jax.ShapeDtypeStruct((T, TOP_K), jnp.int32),
                   jax.ShapeDtypeStruct((T, TOP_K), f32)],
        compiler_params=_cparams(("parallel",)),
        name="ffn_norm_router",
    )(h2d, gain[None, :], wcat, w_hi, b_router[None, :])


def _moe_kernel(te_ref, tr_ref, rowtok_hbm, hp_hbm, wg_ref, wl_ref, woa_ref, wob_ref,
                bg_ref, bl_ref, boa_ref, bob_ref, o_ref,
                xg_ref, act_ref, idx_ref, idx_sem, row_sem, *, n_steps_in, n_steps_out, row_variants):
    i = pl.program_id(0)
    j = pl.program_id(1)
    n_tiles = pl.num_programs(0)
    tm = act_ref.shape[0]
    half = xg_ref.shape[2]
    tn1 = wg_ref.shape[2]
    tn2 = woa_ref.shape[2]
    rows = tr_ref[i]
    rows_prev = tr_ref[jnp.maximum(i - 1, 0)]
    nxt = jnp.minimum(i + 1, n_tiles - 1)
    per_step = tm // n_steps_out

    def idx_copy(t):
        return pltpu.make_async_copy(rowtok_hbm.at[t], idx_ref, idx_sem)

    def row_copy(r, g, u):
        tok = idx_ref[r]
        return pltpu.make_async_copy(hp_hbm.at[tok >> 3, tok & 7], xg_ref.at[g, u], row_sem)

    def gather_wait():
        pltpu.make_async_copy(hp_hbm.at[pl.ds(0, tm // SUBLANES)], xg_ref, row_sem).wait()

    @pl.when(j == 0)
    def _():
        @pl.when(i == 0)
        def _():
            idx_copy(0).start()
            idx_copy(0).wait()

            def body(g, carry):
                for u in range(SUBLANES):
                    row_copy(g * SUBLANES + u, g, u).start()
                return carry
            lax.fori_loop(0, tm // SUBLANES, body, 0)

        @pl.when((i == 0) | (rows_prev > 0))
        def _():
            gather_wait()

        @pl.when(rows > 0)
        def _():
            idx_copy(nxt).start()

    @pl.when((j == n_steps_in - 1) & (rows > 0))
    def _():
        idx_copy(nxt).wait()

    lo = 0
    for m in row_variants:
        in_bucket = (rows > lo) & (rows <= m)
        lo = m

        @pl.when((j < n_steps_in) & in_bucket)
        def _():
            w = xg_ref[:m // SUBLANES].reshape(m, half)
            x = jnp.concatenate([_unpack_lo(w).astype(bf16), _unpack_hi(w).astype(bf16)], axis=1)
            hg = jnp.dot(x, wg_ref[0].astype(bf16), preferred_element_type=f32) + bg_ref[0]
            hl = jnp.dot(x, wl_ref[0].astype(bf16), preferred_element_type=f32) + bl_ref[0]
            hg = jnp.minimum(hg, SWIGLU_LIMIT)
            hl = jnp.clip(hl, -SWIGLU_LIMIT, SWIGLU_LIMIT)
            a = (hg * jax.nn.sigmoid(SWIGLU_ALPHA * hg) * (hl + 1.0)).astype(bf16)
            for jj in range(n_steps_in):
                @pl.when(j == jj)
                def _():
                    act_ref[:m, jj * tn1:(jj + 1) * tn1] = a

        @pl.when((j >= n_steps_in) & in_bucket)
        def _():
            jo = j - n_steps_in
            for r in range(per_step):
                row_copy(jo * per_step + r, jo * (per_step // SUBLANES) + r // SUBLANES, r % SUBLANES).start()
            a = act_ref[:m, :]
            oa = jnp.dot(a, woa_ref[0].astype(bf16), preferred_element_type=f32) + boa_ref[0]
            ob = jnp.dot(a, wob_ref[0].astype(bf16), preferred_element_type=f32) + bob_ref[0]
            o_ref[:m, :] = _pack_bf16_pair(oa, ob)
            if m < tm:
                o_ref[m:, :] = jnp.zeros((tm - m, tn2), u32)

    @pl.when((j >= n_steps_in) & (rows == 0))
    def _():
        o_ref[...] = jnp.zeros_like(o_ref)

    @pl.when((i == n_tiles - 1) & (j == n_steps_in + n_steps_out - 1) & (rows > 0))
    def _():
        gather_wait()


def _moe_tile_rows(T, n_experts):
    avg = T * TOP_K / n_experts
    passes = max(1, -(-int(avg) // MOE_TILE_TARGET))
    return -(-int(avg / passes * MOE_TILE_SLACK) // 64) * 64


def _moe(hp, tile_expert, tile_rows, row_tok, w_in, b_in, w_out, b_out):
    T, half = hp.shape
    D = 2 * half
    E, F = w_out.shape[0], w_out.shape[1]
    n_tiles, tm = row_tok.shape
    tn1 = _pick(F, (256, 128))
    tn2 = _pick(half, (256, 128))
    J1, J2 = F // tn1, half // tn2
    row_variants = (tm // 2, tm)
    assert tm % (J2 * SUBLANES) == 0 and (tm // 2) % (2 * SUBLANES) == 0

    def im(f):
        return lambda i, j, te, tr: f(i, j, te)

    def c1(j):
        return jnp.minimum(j, J1 - 1)

    def c2(j):
        return jnp.maximum(j - J1, 0)

    grid_spec = pltpu.PrefetchScalarGridSpec(
        num_scalar_prefetch=2,
        grid=(n_tiles, J1 + J2),
        in_specs=[
            pl.BlockSpec(memory_space=pl.ANY),
            pl.BlockSpec(memory_space=pl.ANY),
            pl.BlockSpec((1, D, tn1), im(lambda i, j, te: (te[i], 0, c1(j)))),
            pl.BlockSpec((1, D, tn1), im(lambda i, j, te: (te[i], 0, J1 + c1(j)))),
            pl.BlockSpec((1, F, tn2), im(lambda i, j, te: (te[i], 0, c2(j)))),
            pl.BlockSpec((1, F, tn2), im(lambda i, j, te: (te[i], 0, J2 + c2(j)))),
            pl.BlockSpec((1, 1, tn1), im(lambda i, j, te: (te[i], 0, c1(j)))),
            pl.BlockSpec((1, 1, tn1), im(lambda i, j, te: (te[i], 0, J1 + c1(j)))),
            pl.BlockSpec((1, 1, tn2), im(lambda i, j, te: (te[i], 0, c2(j)))),
            pl.BlockSpec((1, 1, tn2), im(lambda i, j, te: (te[i], 0, J2 + c2(j)))),
        ],
        out_specs=pl.BlockSpec((tm, tn2), im(lambda i, j, te: (i, c2(j)))),
        scratch_shapes=[
            pltpu.VMEM((tm // SUBLANES, SUBLANES, half), u32),
            pltpu.VMEM((tm, F), bf16),
            pltpu.SMEM((tm,), jnp.int32),
            pltpu.SemaphoreType.DMA,
            pltpu.SemaphoreType.DMA,
        ],
    )
    b_in3, b_out3 = b_in[:, None, :], b_out[:, None, :]
    return pl.pallas_call(
        functools.partial(_moe_kernel, n_steps_in=J1, n_steps_out=J2, row_variants=row_variants),
        grid_spec=grid_spec,
        out_shape=jax.ShapeDtypeStruct((n_tiles * tm, half), u32),
        compiler_params=_cparams(("arbitrary", "arbitrary")),
        name="moe_experts",
    )(tile_expert, tile_rows, row_tok, hp.reshape(T // SUBLANES, SUBLANES, half),
      w_in, w_in, w_out, w_out, b_in3, b_in3, b_out3, b_out3)


def _combine_kernel(pos_hbm, o_hbm, h_ref, gate_ref, out_ref, buf_ref, idx0_ref, idx1_ref, idx_sem, row_sem):
    i = pl.program_id(0)
    n = pl.num_programs(0)
    tq, D = h_ref.shape
    half = D // 2
    qg = tq // SUBLANES
    idx_refs = (idx0_ref, idx1_ref)

    def idx_copy(t, s):
        return pltpu.make_async_copy(pos_hbm.at[t], idx_refs[s], idx_sem.at[s])

    def row_copy(s, g, u, k):
        r = idx_refs[s][(g * SUBLANES + u) * TOP_K + k]
        return pltpu.make_async_copy(o_hbm.at[r >> 3, r & 7], buf_ref.at[s, k * qg + g, u], row_sem.at[s])

    def gather_start(s):
        def body(g, carry):
            for u in range(SUBLANES):
                for k in range(TOP_K):
                    row_copy(s, g, u, k).start()
            return carry
        lax.fori_loop(0, qg, body, 0)

    def rows_wait(s):
        pltpu.make_async_copy(o_hbm.at[pl.ds(0, TOP_K * qg)], buf_ref.at[s], row_sem.at[s]).wait()

    nxt = jnp.minimum(i + 1, n - 1)

    @pl.when(i == 0)
    def _():
        idx_copy(0, 0).start()
        idx_copy(0, 0).wait()
        gather_start(0)
        idx_copy(nxt, 1).start()

    for s in range(2):
        @pl.when(i % 2 == s)
        def _():
            idx_copy(nxt, 1 - s).wait()
            rows_wait(s)
            gather_start(1 - s)
            gate = gate_ref[...]
            acc_lo = h_ref[:, :half]
            acc_hi = h_ref[:, half:]
            for k in range(TOP_K):
                w = buf_ref[s, k * qg:(k + 1) * qg].reshape(tq, half)
                gk = gate[:, k:k + 1]
                acc_lo = acc_lo + gk * _unpack_lo(w)
                acc_hi = acc_hi + gk * _unpack_hi(w)
            out_ref[:, :half] = acc_lo
            out_ref[:, half:] = acc_hi

            @pl.when(i + 1 < n)
            def _():
                idx_copy(jnp.minimum(i + 2, n - 1), s).start()

            @pl.when(i + 1 == n)
            def _():
                rows_wait(1 - s)


def _combine(h2d, gates, o_packed, pos):
    T, D = h2d.shape
    half = D // 2
    P = o_packed.shape[0]
    tq = min(128, T)
    pos2 = pos.reshape(T // tq, tq * TOP_K)
    return pl.pallas_call(
        _combine_kernel,
        grid=(T // tq,),
        in_specs=[pl.BlockSpec(memory_space=pl.ANY), pl.BlockSpec(memory_space=pl.ANY),
                  pl.BlockSpec((tq, D), lambda i: (i, 0)), pl.BlockSpec((tq, TOP_K), lambda i: (i, 0))],
        out_specs=pl.BlockSpec((tq, D), lambda i: (i, 0)),
        out_shape=jax.ShapeDtypeStruct((T, D), f32),
        scratch_shapes=[pltpu.VMEM((2, TOP_K * tq // SUBLANES, SUBLANES, half), u32),
                        pltpu.SMEM((TOP_K * tq,), jnp.int32),
                        pltpu.SMEM((TOP_K * tq,), jnp.int32),
                        pltpu.SemaphoreType.DMA((2,)),
                        pltpu.SemaphoreType.DMA((2,))],
        compiler_params=_cparams(("arbitrary",)),
        name="moe_combine",
    )(pos2, o_packed.reshape(P // SUBLANES, SUBLANES, half), h2d, gates)


def _routing_tables(top_idx, n_experts, tm):
    T = top_idx.shape[0]
    M = T * TOP_K
    e_flat = top_idx.reshape(-1)
    ids = jnp.arange(M, dtype=jnp.int32)
    _, order = lax.sort((e_flat, ids), num_keys=1, is_stable=True)
    _, inv = lax.sort((order, ids), num_keys=1)
    experts = jnp.arange(n_experts, dtype=jnp.int32)
    counts = jnp.sum((e_flat[:, None] == experts[None, :]).astype(jnp.int32), axis=0)
    starts = jnp.cumsum(counts) - counts
    tiles_per = (counts + tm - 1) // tm
    tile_end = jnp.cumsum(tiles_per)
    tile_start = tile_end - tiles_per
    rank = inv - starts[e_flat]
    pos = (tile_start[e_flat] + rank // tm) * tm + rank % tm
    n_tiles = M // tm + n_experts
    t_ids = jnp.arange(n_tiles, dtype=jnp.int32)
    te = jnp.minimum(jnp.sum((tile_end[None, :] <= t_ids[:, None]).astype(jnp.int32), axis=1), n_experts - 1)
    n_used = tile_end[-1]
    used = t_ids < n_used
    tile_expert = jnp.where(used, te, te[jnp.maximum(n_used - 1, 0)])
    k_in = t_ids - tile_start[te]
    tile_rows = jnp.where(used, jnp.clip(counts[te] - k_in * tm, 0, tm), 0).astype(jnp.int32)
    src = jnp.where(used, starts[te] + k_in * tm, 0)
    tok_sorted = jnp.concatenate([order // TOP_K, jnp.zeros((tm,), jnp.int32)])
    row_tok = jnp.take(tok_sorted, src[:, None] + jnp.arange(tm, dtype=jnp.int32)[None, :], axis=0)
    return tile_expert.astype(jnp.int32), tile_rows, row_tok, pos.reshape(T, TOP_K)


def kernel(x, positions, attn_norm_gain, w_in_proj, swa_q_gain, swa_k_gain, swa_sinks, w_out_proj, ffn_norm_gain,
           w_router, b_router, w_expert_in, b_expert_in, w_expert_out, b_expert_out):
    B, S, D = x.shape
    T = B * S
    depth = w_in_proj.shape[0]
    R = D // 2
    SW = D - R
    n_ret = R // RET_HEAD_DIM
    n_swa = swa_sinks.shape[1]
    KV = (n_swa // SWA_GROUP) * SWA_HEAD_DIM
    n_experts = w_router.shape[2]
    assert w_in_proj.shape[2] == 4 * R + SW + 2 * KV and SW == n_swa * SWA_HEAD_DIM and S % WINDOW == 0

    cos_r, sin_r, cos_s, sin_s = _rope_tables(positions.reshape(T, 1))
    h = x.reshape(T, D)
    for layer in range(depth):
        xn = _rmsnorm_bf16(h, attn_norm_gain[layer])
        proj = _in_proj(xn, w_in_proj[layer].astype(bf16))
        ret = _retention(proj, cos_r, sin_r, B, S, n_ret)
        swa = _swa(proj, cos_s, sin_s, swa_sinks[layer], swa_q_gain[layer], swa_k_gain[layer], B, S, 4 * R, n_swa)
        h = _out_proj(h, ret, swa, w_out_proj[layer].astype(bf16))

        hp, top_idx, gates = _router(h, ffn_norm_gain[layer], w_router[layer], b_router[layer])
        tm = _moe_tile_rows(T, n_experts)
        tile_expert, tile_rows, row_tok, pos = _routing_tables(top_idx, n_experts, tm)
        o_packed = _moe(hp, tile_expert, tile_rows, row_tok,
                        w_expert_in[layer], b_expert_in[layer], w_expert_out[layer], b_expert_out[layer])
        h = _combine(h, gates, o_packed, pos)
    return h.reshape(B, S, D)
```

```python
import functools

import jax
import jax.numpy as jnp
from jax import lax
from jax.experimental import pallas as pl
from jax.experimental.pallas import tpu as pltpu

RET_HEAD_DIM = 256
SWA_HEAD_DIM = 64
SWA_GROUP = 8
WINDOW = 128
ROPE_THETA = 10000.0
TOP_K = 4
SWIGLU_LIMIT = 7.0
SWIGLU_ALPHA = 1.702
NORM_EPS = 1e-5
NEG_INF = -1e30

RET_CHUNK = 256
MOE_TILE_TARGET = 1024
MOE_TILE_SLACK = 1.0625
MOE_WEIGHT_BUFFERS = 3
LANES = 128
SUBLANES = 8
VMEM_LIMIT = 56 * 1024 * 1024

f32 = jnp.float32
bf16 = jnp.bfloat16
u32 = jnp.uint32


def _cparams(sem, vmem=VMEM_LIMIT):
    return pltpu.CompilerParams(dimension_semantics=sem, vmem_limit_bytes=vmem)


def _pick(n, prefs):
    for p in prefs:
        if n % p == 0:
            return p
    return n


def _pack_bf16_pair(lo, hi):
    lo_b = lax.bitcast_convert_type(lo.astype(bf16).astype(f32), u32)
    hi_b = lax.bitcast_convert_type(hi.astype(bf16).astype(f32), u32)
    return hi_b | (lo_b >> 16)


def _unpack_lo(w):
    return lax.bitcast_convert_type(w << 16, f32)


def _unpack_hi(w):
    return lax.bitcast_convert_type(w & jnp.uint32(0xFFFF0000), f32)


def _rope_kernel(pos_ref, fr_ref, fs_ref, ss_ref, cr_ref, snr_ref, cs_ref, sns_ref):
    pos = pos_ref[...].astype(f32)
    ang_r = pos * fr_ref[...]
    cr_ref[...] = jnp.cos(ang_r)
    snr_ref[...] = jnp.sin(ang_r)
    ang_s = pos * fs_ref[...]
    cs_ref[...] = jnp.cos(ang_s)
    sns_ref[...] = jnp.sin(ang_s) * ss_ref[...]


def _rope_tables(pos_flat):
    T = pos_flat.shape[0]
    half = RET_HEAD_DIM // 2
    assert half == LANES
    fr = (1.0 / (ROPE_THETA ** jnp.linspace(0.0, 1.0, half, dtype=f32)))[None, :]
    d = SWA_HEAD_DIM
    inv_s = ROPE_THETA ** (-jnp.arange(0, d, 2, dtype=f32) / d)
    fs = jnp.tile(inv_s, LANES // (d // 2))[None, :]
    sgn = jnp.concatenate([-jnp.ones((d // 2,), f32), jnp.ones((d // 2,), f32)])
    ss = jnp.tile(sgn, LANES // d)[None, :]
    tr = min(512, T)
    row = lambda w: pl.BlockSpec((tr, w), lambda i: (i, 0))
    cst = pl.BlockSpec((1, LANES), lambda i: (0, 0))
    return pl.pallas_call(
        _rope_kernel,
        grid=(T // tr,),
        in_specs=[row(1), cst, cst, cst],
        out_specs=[row(LANES)] * 4,
        out_shape=[jax.ShapeDtypeStruct((T, LANES), f32)] * 4,
        compiler_params=_cparams(("parallel",)),
        name="rope_tables",
    )(pos_flat, fr, fs, ss)


def _rmsnorm_kernel(x_ref, g_ref, o_ref):
    x = x_ref[...]
    y = x * lax.rsqrt(jnp.mean(x * x, axis=-1, keepdims=True) + NORM_EPS)
    o_ref[...] = (y * g_ref[...]).astype(o_ref.dtype)


def _rmsnorm_bf16(x2d, gain):
    T, D = x2d.shape
    tr = min(256, T)
    return pl.pallas_call(
        _rmsnorm_kernel,
        grid=(T // tr,),
        in_specs=[pl.BlockSpec((tr, D), lambda i: (i, 0)), pl.BlockSpec((1, D), lambda i: (0, 0))],
        out_specs=pl.BlockSpec((tr, D), lambda i: (i, 0)),
        out_shape=jax.ShapeDtypeStruct((T, D), bf16),
        compiler_params=_cparams(("parallel",)),
        name="attn_rmsnorm",
    )(x2d, gain[None, :])


def _matmul_kernel(x_ref, w_ref, o_ref):
    o_ref[...] = jnp.dot(x_ref[...], w_ref[...], preferred_element_type=f32).astype(o_ref.dtype)


def _in_proj(xn, w):
    T, D = xn.shape
    N = w.shape[1]
    tm = _pick(T, (1024, 512, 256))
    tn = _pick(N, (768, 512, 256, 128))
    return pl.pallas_call(
        _matmul_kernel,
        grid=(T // tm, N // tn),
        in_specs=[pl.BlockSpec((tm, D), lambda i, j: (i, 0)), pl.BlockSpec((D, tn), lambda i, j: (0, j))],
        out_specs=pl.BlockSpec((tm, tn), lambda i, j: (i, j)),
        out_shape=jax.ShapeDtypeStruct((T, N), bf16),
        compiler_params=_cparams(("parallel", "arbitrary")),
        name="in_proj",
    )(xn, w)


def _retention_kernel(q_ref, k_ref, v_ref, g_ref, cos_ref, sin_ref, dec_ref, xi_ref, zeta_ref, cd_ref,
                      o_ref, state_ref):
    n = pl.program_id(1)
    hd = RET_HEAD_DIM
    n_heads = q_ref.shape[1] // hd

    @pl.when(n == 0)
    def _():
        state_ref[...] = jnp.zeros_like(state_ref)

    c = cos_ref[...]
    s = sin_ref[...]
    cos = jnp.concatenate([c, c], axis=1)
    sin = jnp.concatenate([-s, s], axis=1)

    def rot(x):
        return x * cos + pltpu.roll(x, hd // 2, 1) * sin

    for h in range(n_heads):
        sl = slice(h * hd, (h + 1) * hd)
        qr = rot(q_ref[:, sl].astype(f32))
        kr = rot(k_ref[:, sl].astype(f32)) * (hd ** -0.5)
        v = v_ref[:, sl]
        scores = lax.dot_general(qr.astype(bf16), kr.astype(bf16), (((1,), (1,)), ((), ())),
                                 preferred_element_type=f32)
        scores = scores * dec_ref[h]
        intra = jnp.dot(scores.astype(bf16), v, preferred_element_type=f32)
        state = state_ref[h]
        cross = jnp.dot((qr * xi_ref[h]).astype(bf16), state.astype(bf16), preferred_element_type=f32)
        kz = (kr * zeta_ref[h]).astype(bf16)
        kv = lax.dot_general(kz, v, (((0,), (0,)), ((), ())), preferred_element_type=f32)
        state_ref[h] = state * cd_ref[h] + kv
        o = intra + cross
        o = o * lax.rsqrt(jnp.mean(o * o, axis=-1, keepdims=True) + NORM_EPS)
        g = g_ref[:, sl].astype(f32)
        o_ref[:, sl] = (o * (g * jax.nn.sigmoid(g))).astype(o_ref.dtype)


def _retention(proj, cos_r, sin_r, B, S, n_heads):
    T = proj.shape[0]
    C = min(RET_CHUNK, S)
    NC = S // C
    hd = RET_HEAD_DIM
    R = n_heads * hd
    log_gamma = jnp.log(1.0 - 2.0 ** (-5.0 - jnp.arange(n_heads, dtype=f32)))
    idx = jnp.arange(C, dtype=f32)
    rel = idx[:, None] - idx[None, :]
    decay = jnp.where(rel >= 0, jnp.exp(log_gamma[:, None, None] * jnp.maximum(rel, 0.0)), 0.0)
    xi = jnp.exp(log_gamma[:, None] * (idx + 1.0))[:, :, None]
    zeta = jnp.exp(log_gamma[:, None] * (C - 1.0 - idx))[:, :, None]
    cdec = jnp.exp(log_gamma * C)[:, None, None]

    col = lambda j: pl.BlockSpec((C, R), lambda b, n: (b * NC + n, j))
    tab = pl.BlockSpec((C, LANES), lambda b, n: (b * NC + n, 0))
    per_head = lambda s: pl.BlockSpec((n_heads,) + s, lambda b, n: (0, 0, 0))
    return pl.pallas_call(
        _retention_kernel,
        grid=(B, NC),
        in_specs=[col(0), col(1), col(2), col(3), tab, tab,
                  per_head((C, C)), per_head((C, 1)), per_head((C, 1)), per_head((1, 1))],
        out_specs=pl.BlockSpec((C, R), lambda b, n: (b * NC + n, 0)),
        out_shape=jax.ShapeDtypeStruct((T, R), bf16),
        scratch_shapes=[pltpu.VMEM((n_heads, hd, hd), f32)],
        compiler_params=_cparams(("parallel", "arbitrary")),
        name="retention",
    )(proj, proj, proj, proj, cos_r, sin_r, decay, xi, zeta, cdec)


def _swap32(x):
    w = x.shape[-1]
    lane = lax.broadcasted_iota(jnp.int32, x.shape, 1)
    first = (lane % SWA_HEAD_DIM) < (SWA_HEAD_DIM // 2)
    return jnp.where(first, pltpu.roll(x, w - SWA_HEAD_DIM // 2, 1), pltpu.roll(x, SWA_HEAD_DIM // 2, 1))


def _swa_kernel(sink_ref, q_ref, kp_ref, kc_ref, vp_ref, vc_ref, cosp_ref, sinp_ref, cosc_ref, sinc_ref,
                qg_ref, kg_ref, gmat_ref, o_ref):
    n = pl.program_id(1)
    W = WINDOW
    d = SWA_HEAD_DIM
    gw = SWA_GROUP * d
    kvw = kc_ref.shape[1]
    n_kv = kvw // d
    pairs = gw // LANES

    lane = lax.broadcasted_iota(jnp.int32, (W, LANES), 1)
    is_lo = lane < d
    cosc = cosc_ref[...]
    sinc = sinc_ref[...]

    def rms_rot(x, gain, gsub, cos, sin):
        reps = x.shape[1] // LANES
        ssq = jnp.dot((x * x).astype(bf16), gsub, preferred_element_type=f32)
        xn = x * lax.rsqrt(ssq * (1.0 / d) + NORM_EPS) * gain
        return xn * jnp.concatenate([cos] * reps, axis=1) + _swap32(xn) * jnp.concatenate([sin] * reps, axis=1)

    gk = gmat_ref[:kvw, :kvw]
    krp = rms_rot(kp_ref[...].astype(f32), kg_ref[...], gk, cosp_ref[...], sinp_ref[...])
    krc = rms_rot(kc_ref[...].astype(f32), kg_ref[...], gk, cosc, sinc)
    vp = vp_ref[...].astype(f32)
    vc = vc_ref[...].astype(f32)

    neg_prev = jnp.where(n > 0, 0.0, NEG_INF)
    row = lax.broadcasted_iota(jnp.int32, (2 * W, W), 0) % W
    colj = lax.broadcasted_iota(jnp.int32, (2 * W, W), 1)
    tri = colj <= row
    top = lax.broadcasted_iota(jnp.int32, (2 * W, 1), 0) < W

    for g in range(n_kv):
        c, hi_half = divmod(g, LANES // d)

        def halves(x):
            xc = x[:, c * LANES:(c + 1) * LANES]
            r = pltpu.roll(xc, d, 1)
            return (r, xc) if hi_half else (xc, r)

        def keys(x):
            a, b = halves(x)
            return jnp.where(is_lo, a, b)

        def vals(x):
            a, b = halves(x)
            v_lo = jnp.where(is_lo, a, jnp.where(lane == d, 1.0, 0.0))
            v_hi = jnp.where(is_lo, jnp.where(lane == 0, 1.0, 0.0), b)
            return v_lo, v_hi

        kk = jnp.concatenate([keys(krp), keys(krc)], axis=0).astype(bf16)
        vlo_p, vhi_p = vals(vp)
        vlo_c, vhi_c = vals(vc)
        v_lo = jnp.concatenate([vlo_p, vlo_c], axis=0).astype(bf16)
        v_hi = jnp.concatenate([vhi_p, vhi_c], axis=0).astype(bf16)

        q = q_ref[:, g * gw:(g + 1) * gw].astype(f32)
        qr = rms_rot(q, qg_ref[...], gmat_ref[...], cosc, sinc) * (d ** -0.5)

        for p in range(pairs):
            qp = qr[:, p * LANES:(p + 1) * LANES]
            lhs = jnp.concatenate([jnp.where(is_lo, qp, 0.0), jnp.where(is_lo, 0.0, qp)], axis=0).astype(bf16)
            lg = lax.dot_general(lhs, kk, (((1,), (1,)), ((), ())), preferred_element_type=f32)
            merged = jnp.where(tri, lg[:, W:], lg[:, :W] + neg_prev)
            h0 = g * SWA_GROUP + 2 * p
            sink = jnp.where(top, sink_ref[h0], sink_ref[h0 + 1])
            m = jnp.maximum(jnp.max(merged, axis=-1, keepdims=True), sink)
            eb = jnp.exp(merged - m).astype(bf16)
            zero = jnp.zeros_like(eb)
            probs = jnp.concatenate([jnp.where(tri, zero, eb), jnp.where(tri, eb, zero)], axis=1)
            o_lo = jnp.dot(probs[:W], v_lo, preferred_element_type=f32)
            o_hi = jnp.dot(probs[W:], v_hi, preferred_element_type=f32)
            esink = jnp.exp(sink - m)
            r_lo = 1.0 / (o_lo[:, d:d + 1] + esink[:W])
            r_hi = 1.0 / (o_hi[:, 0:1] + esink[W:])
            out = jnp.where(is_lo, o_lo * r_lo, o_hi * r_hi)
            o_ref[:, g * gw + p * LANES:g * gw + (p + 1) * LANES] = out.astype(o_ref.dtype)


def _swa(proj, cos_s, sin_s, sinks, q_gain, k_gain, B, S, q_off, n_heads):
    T = proj.shape[0]
    W = WINDOW
    NB = S // W
    d = SWA_HEAD_DIM
    n_kv = n_heads // SWA_GROUP
    SW, KV = n_heads * d, n_kv * d
    gw = SWA_GROUP * d
    assert q_off % SW == 0 and (q_off + SW) % KV == 0 and KV % LANES == 0 and KV <= gw
    qg = jnp.tile(q_gain.astype(f32), SWA_GROUP)[None, :]
    kg = jnp.tile(k_gain.astype(f32), n_kv)[None, :]
    head_id = jnp.arange(gw) // d
    gmat = (head_id[:, None] == head_id[None, :]).astype(bf16)
    qb = q_off // SW
    kb = (q_off + SW) // KV

    def im(f):
        return lambda b, n, s: f(b, n)

    cur = lambda w, j: pl.BlockSpec((W, w), im(lambda b, n: (b * NB + n, j)))
    prev = lambda w, j: pl.BlockSpec((W, w), im(lambda b, n: (b * NB + jnp.maximum(n - 1, 0), j)))
    cst = lambda r, w: pl.BlockSpec((r, w), im(lambda b, n: (0, 0)))
    grid_spec = pltpu.PrefetchScalarGridSpec(
        num_scalar_prefetch=1,
        grid=(B, NB),
        in_specs=[cur(SW, qb), prev(KV, kb), cur(KV, kb), prev(KV, kb + 1), cur(KV, kb + 1),
                  prev(LANES, 0), prev(LANES, 0), cur(LANES, 0), cur(LANES, 0),
                  cst(1, gw), cst(1, KV), cst(gw, gw)],
        out_specs=cur(SW, 0),
    )
    return pl.pallas_call(
        _swa_kernel,
        grid_spec=grid_spec,
        out_shape=jax.ShapeDtypeStruct((T, SW), bf16),
        compiler_params=_cparams(("parallel", "parallel")),
        name="swa",
    )(sinks.astype(f32), proj, proj, proj, proj, proj, cos_s, sin_s, cos_s, sin_s, qg, kg, gmat)


def _out_proj_kernel(x_ref, a_ref, b_ref, wa_ref, wb_ref, o_ref):
    acc = jnp.dot(a_ref[...], wa_ref[...], preferred_element_type=f32)
    acc = acc + jnp.dot(b_ref[...], wb_ref[...], preferred_element_type=f32)
    o_ref[...] = x_ref[...] + acc


def _out_proj(x2d, ret, swa, w):
    T, D = x2d.shape
    R, SW = ret.shape[1], swa.shape[1]
    assert R == SW
    tm = _pick(T, (1024, 512, 256))
    tn = _pick(D, (512, 256, 128))
    return pl.pallas_call(
        _out_proj_kernel,
        grid=(T // tm, D // tn),
        in_specs=[pl.BlockSpec((tm, tn), lambda i, j: (i, j)),
                  pl.BlockSpec((tm, R), lambda i, j: (i, 0)),
                  pl.BlockSpec((tm, SW), lambda i, j: (i, 0)),
                  pl.BlockSpec((R, tn), lambda i, j: (0, j)),
                  pl.BlockSpec((SW, tn), lambda i, j: (1, j))],
        out_specs=pl.BlockSpec((tm, tn), lambda i, j: (i, j)),
        out_shape=jax.ShapeDtypeStruct((T, D), f32),
        compiler_params=_cparams(("parallel", "arbitrary")),
        name="out_proj",
    )(x2d, ret, swa, w, w)


def _router_kernel(h_ref, g_ref, wcat_ref, whi_ref, br_ref, hp_ref, idx_ref, gate_ref):
    h = h_ref[...]
    hn = h * lax.rsqrt(jnp.mean(h * h, axis=-1, keepdims=True) + NORM_EPS) * g_ref[...]
    E = whi_ref.shape[1]
    hn_hi = hn.astype(bf16)
    hn_hi32 = hn_hi.astype(f32)
    hn_lo = (hn - hn_hi32).astype(bf16)
    l1 = jnp.dot(hn_hi, wcat_ref[...], preferred_element_type=f32)
    l2 = jnp.dot(hn_lo, whi_ref[...], preferred_element_type=f32)
    logits = l1[:, :E] + l1[:, E:] + l2 + br_ref[...]
    lane = lax.broadcasted_iota(jnp.int32, logits.shape, 1)
    vals, idxs = [], []
    cur = logits
    for _ in range(TOP_K):
        m = jnp.max(cur, axis=-1, keepdims=True)
        i = jnp.min(jnp.where(cur == m, lane, E), axis=-1, keepdims=True)
        vals.append(m)
        idxs.append(i)
        cur = jnp.where(lane == i, -jnp.inf, cur)
    ex = [jnp.exp(v - vals[0]) for v in vals]
    tot = ex[0]
    for e in ex[1:]:
        tot = tot + e
    col = lax.broadcasted_iota(jnp.int32, idx_ref.shape, 1)
    idx_out = jnp.zeros(idx_ref.shape, jnp.int32)
    gate_out = jnp.zeros(gate_ref.shape, f32)
    for k in range(TOP_K):
        idx_out = jnp.where(col == k, idxs[k], idx_out)
        gate_out = jnp.where(col == k, ex[k] / tot, gate_out)
    idx_ref[...] = idx_out
    gate_ref[...] = gate_out
    half = hn.shape[1] // 2
    hp_ref[...] = _pack_bf16_pair(hn[:, :half], hn[:, half:])


def _router(h2d, gain, w_router, b_router):
    T, D = h2d.shape
    E = w_router.shape[1]
    w_hi = w_router.astype(bf16)
    w_lo = (w_router - w_hi.astype(f32)).astype(bf16)
    wcat = jnp.concatenate([w_hi, w_lo], axis=1)
    tr = min(256, T)
    row = lambda w: pl.BlockSpec((tr, w), lambda i: (i, 0))
    cst = lambda r, w: pl.BlockSpec((r, w), lambda i: (0, 0))
    return pl.pallas_call(
        _router_kernel,
        grid=(T // tr,),
        in_specs=[row(D), cst(1, D), cst(D, 2 * E), cst(D, E), cst(1, E)],
        out_specs=[row(D // 2), row(TOP_K), row(TOP_K)],
        out_shape=[jax.ShapeDtypeStruct((T, D // 2), u32),
                   jax.ShapeDtypeStruct((T, TOP_K), jnp.int32),
                   jax.ShapeDtypeStruct((T, TOP_K), f32)],
        compiler_params=_cparams(("parallel",)),
        name="ffn_norm_router",
    )(h2d, gain[None, :], wcat, w_hi, b_router[None, :])


def _moe_kernel(te_ref, tr_ref, rowtok_hbm, hp_hbm, win_hbm, wout_hbm,
                bg_ref, bl_ref, boa_ref, bob_ref, o_ref,
                xg_ref, act_ref, wbuf_ref, idx_ref, idx_sem, row_sem, w_sem,
                *, n_steps_in, n_steps_out, row_variants):
    i = pl.program_id(0)
    j = pl.program_id(1)
    n_tiles = pl.num_programs(0)
    tm, F = act_ref.shape
    half = xg_ref.shape[2]
    n_buf, D, tn = wbuf_ref.shape[0], wbuf_ref.shape[1], wbuf_ref.shape[2] // 2
    tn1 = tn2 = tn
    n_steps = n_steps_in + n_steps_out
    rows = tr_ref[i]
    rows_prev = tr_ref[jnp.maximum(i - 1, 0)]
    nxt = jnp.minimum(i + 1, n_tiles - 1)
    per_step = tm // n_steps_out

    def w_copies(e, jt, slot):
        jo = jnp.maximum(jt - n_steps_in, 0)
        ji = jnp.minimum(jt, n_steps_in - 1)
        sem = w_sem.at[slot]
        in_pair = [pltpu.make_async_copy(win_hbm.at[e, :, pl.ds(pl.multiple_of(c * F + ji * tn, tn), tn)],
                                         wbuf_ref.at[slot, :, pl.ds(c * tn, tn)], sem) for c in range(2)]
        out_pair = [pltpu.make_async_copy(wout_hbm.at[e, :, pl.ds(pl.multiple_of(c * half + jo * tn, tn), tn)],
                                          wbuf_ref.at[slot, pl.ds(0, F), pl.ds(c * tn, tn)], sem) for c in range(2)]
        return in_pair, out_pair

    def w_step(p, start):
        it = p // n_steps
        jt = p - it * n_steps
        it_c = jnp.minimum(it, n_tiles - 1)

        @pl.when((it < n_tiles) & (tr_ref[it_c] > 0))
        def _():
            in_pair, out_pair = w_copies(te_ref[it_c], jt, p % n_buf)
            for cond, pair in ((jt < n_steps_in, in_pair), (jt >= n_steps_in, out_pair)):
                @pl.when(cond)
                def _():
                    for cp in pair:
                        cp.start() if start else cp.wait()

    p = i * n_steps + j

    @pl.when(p == 0)
    def _():
        for q in range(n_buf - 1):
            w_step(q, True)

    w_step(p + n_buf - 1, True)
    w_step(p, False)
    slot = p % n_buf

    def idx_copy(t):
        return pltpu.make_async_copy(rowtok_hbm.at[t], idx_ref, idx_sem)

    def row_copy(r, g, u):
        tok = idx_ref[r]
        return pltpu.make_async_copy(hp_hbm.at[tok >> 3, tok & 7], xg_ref.at[g, u], row_sem)

    def gather_wait():
        pltpu.make_async_copy(hp_hbm.at[pl.ds(0, tm // SUBLANES)], xg_ref, row_sem).wait()

    @pl.when(j == 0)
    def _():
        @pl.when(i == 0)
        def _():
            idx_copy(0).start()
            idx_copy(0).wait()

            def body(g, carry):
                for u in range(SUBLANES):
                    row_copy(g * SUBLANES + u, g, u).start()
                return carry
            lax.fori_loop(0, tm // SUBLANES, body, 0)

        @pl.when((i == 0) | (rows_prev > 0))
        def _():
            gather_wait()

        @pl.when(rows > 0)
        def _():
            idx_copy(nxt).start()

    @pl.when((j == n_steps_in - 1) & (rows > 0))
    def _():
        idx_copy(nxt).wait()

    lo = 0
    for m in row_variants:
        in_bucket = (rows > lo) & (rows <= m)
        lo = m

        @pl.when((j < n_steps_in) & in_bucket)
        def _():
            w = xg_ref[:m // SUBLANES].reshape(m, half)
            x = jnp.concatenate([_unpack_lo(w).astype(bf16), _unpack_hi(w).astype(bf16)], axis=1)
            hg = jnp.dot(x, wbuf_ref[slot, :, :tn].astype(bf16), preferred_element_type=f32) + bg_ref[0]
            hl = jnp.dot(x, wbuf_ref[slot, :, tn:].astype(bf16), preferred_element_type=f32) + bl_ref[0]
            hg = jnp.minimum(hg, SWIGLU_LIMIT)
            hl = jnp.clip(hl, -SWIGLU_LIMIT, SWIGLU_LIMIT)
            a = (hg * jax.nn.sigmoid(SWIGLU_ALPHA * hg) * (hl + 1.0)).astype(bf16)
            for jj in range(n_steps_in):
                @pl.when(j == jj)
                def _():
                    act_ref[:m, jj * tn1:(jj + 1) * tn1] = a

        @pl.when((j >= n_steps_in) & in_bucket)
        def _():
            jo = j - n_steps_in
            for r in range(per_step):
                row_copy(jo * per_step + r, jo * (per_step // SUBLANES) + r // SUBLANES, r % SUBLANES).start()
            a = act_ref[:m, :]
            oa = jnp.dot(a, wbuf_ref[slot, :F, :tn].astype(bf16), preferred_element_type=f32) + boa_ref[0]
            ob = jnp.dot(a, wbuf_ref[slot, :F, tn:].astype(bf16), preferred_element_type=f32) + bob_ref[0]
            o_ref[:m, :] = _pack_bf16_pair(oa, ob)
            if m < tm:
                o_ref[m:, :] = jnp.zeros((tm - m, tn2), u32)

    @pl.when((j >= n_steps_in) & (rows == 0))
    def _():
        o_ref[...] = jnp.zeros_like(o_ref)

    @pl.when((i == n_tiles - 1) & (j == n_steps_in + n_steps_out - 1) & (rows > 0))
    def _():
        gather_wait()


def _moe_tile_rows(T, n_experts):
    avg = T * TOP_K / n_experts
    passes = max(1, -(-int(avg) // MOE_TILE_TARGET))
    return -(-int(avg / passes * MOE_TILE_SLACK) // 64) * 64


def _moe(hp, tile_expert, tile_rows, row_tok, w_in, b_in, w_out, b_out):
    T, half = hp.shape
    D = 2 * half
    E, F = w_out.shape[0], w_out.shape[1]
    n_tiles, tm = row_tok.shape
    tn1 = _pick(F, (256, 128))
    tn2 = _pick(half, (256, 128))
    J1, J2 = F // tn1, half // tn2
    row_variants = (tm // 2, tm)
    assert tm % (J2 * SUBLANES) == 0 and (tm // 2) % (2 * SUBLANES) == 0
    assert tn1 == tn2 and F <= D and w_in.shape == (E, D, 2 * F) and w_out.shape == (E, F, D)

    def im(f):
        return lambda i, j, te, tr: f(i, j, te)

    def c1(j):
        return jnp.minimum(j, J1 - 1)

    def c2(j):
        return jnp.maximum(j - J1, 0)

    grid_spec = pltpu.PrefetchScalarGridSpec(
        num_scalar_prefetch=2,
        grid=(n_tiles, J1 + J2),
        in_specs=[
            pl.BlockSpec(memory_space=pl.ANY),
            pl.BlockSpec(memory_space=pl.ANY),
            pl.BlockSpec(memory_space=pl.ANY),
            pl.BlockSpec(memory_space=pl.ANY),
            pl.BlockSpec((1, 1, tn1), im(lambda i, j, te: (te[i], 0, c1(j)))),
            pl.BlockSpec((1, 1, tn1), im(lambda i, j, te: (te[i], 0, J1 + c1(j)))),
            pl.BlockSpec((1, 1, tn2), im(lambda i, j, te: (te[i], 0, c2(j)))),
            pl.BlockSpec((1, 1, tn2), im(lambda i, j, te: (te[i], 0, J2 + c2(j)))),
        ],
        out_specs=pl.BlockSpec((tm, tn2), im(lambda i, j, te: (i, c2(j)))),
        scratch_shapes=[
            pltpu.VMEM((tm // SUBLANES, SUBLANES, half), u32),
            pltpu.VMEM((tm, F), bf16),
            pltpu.VMEM((MOE_WEIGHT_BUFFERS, D, 2 * tn1), f32),
            pltpu.SMEM((tm,), jnp.int32),
            pltpu.SemaphoreType.DMA,
            pltpu.SemaphoreType.DMA,
            pltpu.SemaphoreType.DMA((MOE_WEIGHT_BUFFERS,)),
        ],
    )
    b_in3, b_out3 = b_in[:, None, :], b_out[:, None, :]
    return pl.pallas_call(
        functools.partial(_moe_kernel, n_steps_in=J1, n_steps_out=J2, row_variants=row_variants),
        grid_spec=grid_spec,
        out_shape=jax.ShapeDtypeStruct((n_tiles * tm, half), u32),
        compiler_params=_cparams(("arbitrary", "arbitrary")),
        name="moe_experts",
    )(tile_expert, tile_rows, row_tok, hp.reshape(T // SUBLANES, SUBLANES, half),
      w_in, w_out, b_in3, b_in3, b_out3, b_out3)


def _combine_kernel(pos_hbm, o_hbm, h_ref, gate_ref, out_ref, buf_ref, idx0_ref, idx1_ref, idx_sem, row_sem):
    i = pl.program_id(0)
    n = pl.num_programs(0)
    tq, D = h_ref.shape
    half = D // 2
    qg = tq // SUBLANES
    idx_refs = (idx0_ref, idx1_ref)

    def idx_copy(t, s):
        return pltpu.make_async_copy(pos_hbm.at[t], idx_refs[s], idx_sem.at[s])

    def row_copy(s, g, u, k):
        r = idx_refs[s][(g * SUBLANES + u) * TOP_K + k]
        return pltpu.make_async_copy(o_hbm.at[r >> 3, r & 7], buf_ref.at[s, k * qg + g, u], row_sem.at[s])

    def gather_start(s):
        def body(g, carry):
            for u in range(SUBLANES):
                for k in range(TOP_K):
                    row_copy(s, g, u, k).start()
            return carry
        lax.fori_loop(0, qg, body, 0)

    def rows_wait(s):
        pltpu.make_async_copy(o_hbm.at[pl.ds(0, TOP_K * qg)], buf_ref.at[s], row_sem.at[s]).wait()

    nxt = jnp.minimum(i + 1, n - 1)

    @pl.when(i == 0)
    def _():
        idx_copy(0, 0).start()
        idx_copy(0, 0).wait()
        gather_start(0)
        idx_copy(nxt, 1).start()

    for s in range(2):
        @pl.when(i % 2 == s)
        def _():
            idx_copy(nxt, 1 - s).wait()
            rows_wait(s)
            gather_start(1 - s)
            gate = gate_ref[...]
            acc_lo = h_ref[:, :half]
            acc_hi = h_ref[:, half:]
            for k in range(TOP_K):
                w = buf_ref[s, k * qg:(k + 1) * qg].reshape(tq, half)
                gk = gate[:, k:k + 1]
                acc_lo = acc_lo + gk * _unpack_lo(w)
                acc_hi = acc_hi + gk * _unpack_hi(w)
            out_ref[:, :half] = acc_lo
            out_ref[:, half:] = acc_hi

            @pl.when(i + 1 < n)
            def _():
                idx_copy(jnp.minimum(i + 2, n - 1), s).start()

            @pl.when(i + 1 == n)
            def _():
                rows_wait(1 - s)


def _combine(h2d, gates, o_packed, pos):
    T, D = h2d.shape
    half = D // 2
    P = o_packed.shape[0]
    tq = min(128, T)
    pos2 = pos.reshape(T // tq, tq * TOP_K)
    return pl.pallas_call(
        _combine_kernel,
        grid=(T // tq,),
        in_specs=[pl.BlockSpec(memory_space=pl.ANY), pl.BlockSpec(memory_space=pl.ANY),
                  pl.BlockSpec((tq, D), lambda i: (i, 0)), pl.BlockSpec((tq, TOP_K), lambda i: (i, 0))],
        out_specs=pl.BlockSpec((tq, D), lambda i: (i, 0)),
        out_shape=jax.ShapeDtypeStruct((T, D), f32),
        scratch_shapes=[pltpu.VMEM((2, TOP_K * tq // SUBLANES, SUBLANES, half), u32),
                        pltpu.SMEM((TOP_K * tq,), jnp.int32),
                        pltpu.SMEM((TOP_K * tq,), jnp.int32),
                        pltpu.SemaphoreType.DMA((2,)),
                        pltpu.SemaphoreType.DMA((2,))],
        compiler_params=_cparams(("arbitrary",)),
        name="moe_combine",
    )(pos2, o_packed.reshape(P // SUBLANES, SUBLANES, half), h2d, gates)


def _routing_tables(top_idx, n_experts, tm):
    T = top_idx.shape[0]
    M = T * TOP_K
    e_flat = top_idx.reshape(-1)
    ids = jnp.arange(M, dtype=jnp.int32)
    _, order = lax.sort((e_flat, ids), num_keys=1, is_stable=True)
    _, inv = lax.sort((order, ids), num_keys=1)
    experts = jnp.arange(n_experts, dtype=jnp.int32)
    counts = jnp.sum((e_flat[:, None] == experts[None, :]).astype(jnp.int32), axis=0)
    starts = jnp.cumsum(counts) - counts
    tiles_per = (counts + tm - 1) // tm
    tile_end = jnp.cumsum(tiles_per)
    tile_start = tile_end - tiles_per
    rank = inv - starts[e_flat]
    pos = (tile_start[e_flat] + rank // tm) * tm + rank % tm
    n_tiles = M // tm + n_experts
    t_ids = jnp.arange(n_tiles, dtype=jnp.int32)
    te = jnp.minimum(jnp.sum((tile_end[None, :] <= t_ids[:, None]).astype(jnp.int32), axis=1), n_experts - 1)
    n_used = tile_end[-1]
    used = t_ids < n_used
    tile_expert = jnp.where(used, te, te[jnp.maximum(n_used - 1, 0)])
    k_in = t_ids - tile_start[te]
    tile_rows = jnp.where(used, jnp.clip(counts[te] - k_in * tm, 0, tm), 0).astype(jnp.int32)
    src = jnp.where(used, starts[te] + k_in * tm, 0)
    tok_sorted = jnp.concatenate([order // TOP_K, jnp.zeros((tm,), jnp.int32)])
    row_tok = jnp.take(tok_sorted, src[:, None] + jnp.arange(tm, dtype=jnp.int32)[None, :], axis=0)
    return tile_expert.astype(jnp.int32), tile_rows, row_tok, pos.reshape(T, TOP_K)


def kernel(x, positions, attn_norm_gain, w_in_proj, swa_q_gain, swa_k_gain, swa_sinks, w_out_proj, ffn_norm_gain,
           w_router, b_router, w_expert_in, b_expert_in, w_expert_out, b_expert_out):
    B, S, D = x.shape
    T = B * S
    depth = w_in_proj.shape[0]
    R = D // 2
    SW = D - R
    n_ret = R // RET_HEAD_DIM
    n_swa = swa_sinks.shape[1]
    KV = (n_swa // SWA_GROUP) * SWA_HEAD_DIM
    n_experts = w_router.shape[2]
    assert w_in_proj.shape[2] == 4 * R + SW + 2 * KV and SW == n_swa * SWA_HEAD_DIM and S % WINDOW == 0

    cos_r, sin_r, cos_s, sin_s = _rope_tables(positions.reshape(T, 1))
    h = x.reshape(T, D)
    for layer in range(depth):
        xn = _rmsnorm_bf16(h, attn_norm_gain[layer])
        proj = _in_proj(xn, w_in_proj[layer].astype(bf16))
        ret = _retention(proj, cos_r, sin_r, B, S, n_ret)
        swa = _swa(proj, cos_s, sin_s, swa_sinks[layer], swa_q_gain[layer], swa_k_gain[layer], B, S, 4 * R, n_swa)
        h = _out_proj(h, ret, swa, w_out_proj[layer].astype(bf16))

        hp, top_idx, gates = _router(h, ffn_norm_gain[layer], w_router[layer], b_router[layer])
        tm = _moe_tile_rows(T, n_experts)
        tile_expert, tile_rows, row_tok, pos = _routing_tables(top_idx, n_experts, tm)
        o_packed = _moe(hp, tile_expert, tile_rows, row_tok,
                        w_expert_in[layer], b_expert_in[layer], w_expert_out[layer], b_expert_out[layer])
        h = _combine(h, gates, o_packed, pos)
    return h.reshape(B, S, D)
```

```python
import functools

import jax
import jax.numpy as jnp
from jax import lax
from jax.experimental import pallas as pl
from jax.experimental.pallas import tpu as pltpu

RET_HEAD_DIM = 256
SWA_HEAD_DIM = 64
SWA_GROUP = 8
WINDOW = 128
ROPE_THETA = 10000.0
TOP_K = 4
SWIGLU_LIMIT = 7.0
SWIGLU_ALPHA = 1.702
NORM_EPS = 1e-5
NEG_INF = -1e30

RET_CHUNK = 256
MOE_TILE_TARGET = 1024
MOE_TILE_SLACK = 1.0625
MOE_WEIGHT_BUFFERS = 3
LANES = 128
SUBLANES = 8
VMEM_LIMIT = 56 * 1024 * 1024

f32 = jnp.float32
bf16 = jnp.bfloat16
u32 = jnp.uint32


def _cparams(sem, vmem=VMEM_LIMIT):
    return pltpu.CompilerParams(dimension_semantics=sem, vmem_limit_bytes=vmem)


def _pick(n, prefs):
    for p in prefs:
        if n % p == 0:
            return p
    return n


def _pack_bf16_pair(lo, hi):
    lo_b = lax.bitcast_convert_type(lo.astype(bf16).astype(f32), u32)
    hi_b = lax.bitcast_convert_type(hi.astype(bf16).astype(f32), u32)
    return hi_b | (lo_b >> 16)


def _unpack_lo(w):
    return lax.bitcast_convert_type(w << 16, f32)


def _unpack_hi(w):
    return lax.bitcast_convert_type(w & jnp.uint32(0xFFFF0000), f32)


def _rope_kernel(pos_ref, fr_ref, fs_ref, ss_ref, cr_ref, snr_ref, cs_ref, sns_ref):
    pos = pos_ref[...].astype(f32)
    ang_r = pos * fr_ref[...]
    cr_ref[...] = jnp.cos(ang_r)
    snr_ref[...] = jnp.sin(ang_r)
    ang_s = pos * fs_ref[...]
    cs_ref[...] = jnp.cos(ang_s)
    sns_ref[...] = jnp.sin(ang_s) * ss_ref[...]


def _rope_tables(pos_flat):
    T = pos_flat.shape[0]
    half = RET_HEAD_DIM // 2
    assert half == LANES
    fr = (1.0 / (ROPE_THETA ** jnp.linspace(0.0, 1.0, half, dtype=f32)))[None, :]
    d = SWA_HEAD_DIM
    inv_s = ROPE_THETA ** (-jnp.arange(0, d, 2, dtype=f32) / d)
    fs = jnp.tile(inv_s, LANES // (d // 2))[None, :]
    sgn = jnp.concatenate([-jnp.ones((d // 2,), f32), jnp.ones((d // 2,), f32)])
    ss = jnp.tile(sgn, LANES // d)[None, :]
    tr = min(512, T)
    row = lambda w: pl.BlockSpec((tr, w), lambda i: (i, 0))
    cst = pl.BlockSpec((1, LANES), lambda i: (0, 0))
    return pl.pallas_call(
        _rope_kernel,
        grid=(T // tr,),
        in_specs=[row(1), cst, cst, cst],
        out_specs=[row(LANES)] * 4,
        out_shape=[jax.ShapeDtypeStruct((T, LANES), f32)] * 4,
        compiler_params=_cparams(("parallel",)),
        name="rope_tables",
    )(pos_flat, fr, fs, ss)


def _rmsnorm_kernel(x_ref, g_ref, o_ref):
    x = x_ref[...]
    y = x * lax.rsqrt(jnp.mean(x * x, axis=-1, keepdims=True) + NORM_EPS)
    o_ref[...] = (y * g_ref[...]).astype(o_ref.dtype)


def _rmsnorm_bf16(x2d, gain):
    T, D = x2d.shape
    tr = min(256, T)
    return pl.pallas_call(
        _rmsnorm_kernel,
        grid=(T // tr,),
        in_specs=[pl.BlockSpec((tr, D), lambda i: (i, 0)), pl.BlockSpec((1, D), lambda i: (0, 0))],
        out_specs=pl.BlockSpec((tr, D), lambda i: (i, 0)),
        out_shape=jax.ShapeDtypeStruct((T, D), bf16),
        compiler_params=_cparams(("parallel",)),
        name="attn_rmsnorm",
    )(x2d, gain[None, :])


def _matmul_kernel(x_ref, w_ref, o_ref):
    o_ref[...] = jnp.dot(x_ref[...], w_ref[...], preferred_element_type=f32).astype(o_ref.dtype)


def _in_proj(xn, w):
    T, D = xn.shape
    N = w.shape[1]
    tm = _pick(T, (1024, 512, 256))
    tn = _pick(N, (768, 512, 256, 128))
    return pl.pallas_call(
        _matmul_kernel,
        grid=(T // tm, N // tn),
        in_specs=[pl.BlockSpec((tm, D), lambda i, j: (i, 0)), pl.BlockSpec((D, tn), lambda i, j: (0, j))],
        out_specs=pl.BlockSpec((tm, tn), lambda i, j: (i, j)),
        out_shape=jax.ShapeDtypeStruct((T, N), bf16),
        compiler_params=_cparams(("parallel", "arbitrary")),
        name="in_proj",
    )(xn, w)


def _retention_kernel(q_ref, k_ref, v_ref, g_ref, cos_ref, sin_ref, dec_ref, xi_ref, zeta_ref, cd_ref,
                      o_ref, state_ref):
    n = pl.program_id(1)
    hd = RET_HEAD_DIM
    n_heads = q_ref.shape[1] // hd

    @pl.when(n == 0)
    def _():
        state_ref[...] = jnp.zeros_like(state_ref)

    c = cos_ref[...]
    s = sin_ref[...]
    cos = jnp.concatenate([c, c], axis=1)
    sin = jnp.concatenate([-s, s], axis=1)

    def rot(x):
        return x * cos + pltpu.roll(x, hd // 2, 1) * sin

    for h in range(n_heads):
        sl = slice(h * hd, (h + 1) * hd)
        qr = rot(q_ref[:, sl].astype(f32))
        kr = rot(k_ref[:, sl].astype(f32)) * (hd ** -0.5)
        v = v_ref[:, sl]
        scores = lax.dot_general(qr.astype(bf16), kr.astype(bf16), (((1,), (1,)), ((), ())),
                                 preferred_element_type=f32)
        scores = scores * dec_ref[h]
        intra = jnp.dot(scores.astype(bf16), v, preferred_element_type=f32)
        state = state_ref[h]
        cross = jnp.dot((qr * xi_ref[h]).astype(bf16), state.astype(bf16), preferred_element_type=f32)
        kz = (kr * zeta_ref[h]).astype(bf16)
        kv = lax.dot_general(kz, v, (((0,), (0,)), ((), ())), preferred_element_type=f32)
        state_ref[h] = state * cd_ref[h] + kv
        o = intra + cross
        o = o * lax.rsqrt(jnp.mean(o * o, axis=-1, keepdims=True) + NORM_EPS)
        g = g_ref[:, sl].astype(f32)
        o_ref[:, sl] = (o * (g * jax.nn.sigmoid(g))).astype(o_ref.dtype)


def _retention(proj, cos_r, sin_r, B, S, n_heads):
    T = proj.shape[0]
    C = min(RET_CHUNK, S)
    NC = S // C
    hd = RET_HEAD_DIM
    R = n_heads * hd
    log_gamma = jnp.log(1.0 - 2.0 ** (-5.0 - jnp.arange(n_heads, dtype=f32)))
    idx = jnp.arange(C, dtype=f32)
    rel = idx[:, None] - idx[None, :]
    decay = jnp.where(rel >= 0, jnp.exp(log_gamma[:, None, None] * jnp.maximum(rel, 0.0)), 0.0)
    xi = jnp.exp(log_gamma[:, None] * (idx + 1.0))[:, :, None]
    zeta = jnp.exp(log_gamma[:, None] * (C - 1.0 - idx))[:, :, None]
    cdec = jnp.exp(log_gamma * C)[:, None, None]

    col = lambda j: pl.BlockSpec((C, R), lambda b, n: (b * NC + n, j))
    tab = pl.BlockSpec((C, LANES), lambda b, n: (b * NC + n, 0))
    per_head = lambda s: pl.BlockSpec((n_heads,) + s, lambda b, n: (0, 0, 0))
    return pl.pallas_call(
        _retention_kernel,
        grid=(B, NC),
        in_specs=[col(0), col(1), col(2), col(3), tab, tab,
                  per_head((C, C)), per_head((C, 1)), per_head((C, 1)), per_head((1, 1))],
        out_specs=pl.BlockSpec((C, R), lambda b, n: (b * NC + n, 0)),
        out_shape=jax.ShapeDtypeStruct((T, R), bf16),
        scratch_shapes=[pltpu.VMEM((n_heads, hd, hd), f32)],
        compiler_params=_cparams(("parallel", "arbitrary")),
        name="retention",
    )(proj, proj, proj, proj, cos_r, sin_r, decay, xi, zeta, cdec)


def _swap32(x):
    w = x.shape[-1]
    lane = lax.broadcasted_iota(jnp.int32, x.shape, 1)
    first = (lane % SWA_HEAD_DIM) < (SWA_HEAD_DIM // 2)
    return jnp.where(first, pltpu.roll(x, w - SWA_HEAD_DIM // 2, 1), pltpu.roll(x, SWA_HEAD_DIM // 2, 1))


def _swa_kernel(sink_ref, q_ref, kp_ref, kc_ref, vp_ref, vc_ref, cosp_ref, sinp_ref, cosc_ref, sinc_ref,
                qg_ref, kg_ref, gmat_ref, o_ref):
    n = pl.program_id(1)
    W = WINDOW
    d = SWA_HEAD_DIM
    gw = SWA_GROUP * d
    kvw = kc_ref.shape[1]
    n_kv = kvw // d
    pairs = gw // LANES

    lane = lax.broadcasted_iota(jnp.int32, (W, LANES), 1)
    is_lo = lane < d
    cosc = cosc_ref[...]
    sinc = sinc_ref[...]

    def rms_rot(x, gain, gsub, cos, sin):
        reps = x.shape[1] // LANES
        ssq = jnp.dot((x * x).astype(bf16), gsub, preferred_element_type=f32)
        xn = x * lax.rsqrt(ssq * (1.0 / d) + NORM_EPS) * gain
        return xn * jnp.concatenate([cos] * reps, axis=1) + _swap32(xn) * jnp.concatenate([sin] * reps, axis=1)

    gk = gmat_ref[:kvw, :kvw]
    krp = rms_rot(kp_ref[...].astype(f32), kg_ref[...], gk, cosp_ref[...], sinp_ref[...])
    krc = rms_rot(kc_ref[...].astype(f32), kg_ref[...], gk, cosc, sinc)
    vp = vp_ref[...].astype(f32)
    vc = vc_ref[...].astype(f32)

    neg_prev = jnp.where(n > 0, 0.0, NEG_INF)
    row = lax.broadcasted_iota(jnp.int32, (2 * W, W), 0) % W
    colj = lax.broadcasted_iota(jnp.int32, (2 * W, W), 1)
    tri = colj <= row
    top = lax.broadcasted_iota(jnp.int32, (2 * W, 1), 0) < W

    for g in range(n_kv):
        c, hi_half = divmod(g, LANES // d)

        def halves(x):
            xc = x[:, c * LANES:(c + 1) * LANES]
            r = pltpu.roll(xc, d, 1)
            return (r, xc) if hi_half else (xc, r)

        def keys(x):
            a, b = halves(x)
            return jnp.where(is_lo, a, b)

        def vals(x):
            a, b = halves(x)
            v_lo = jnp.where(is_lo, a, jnp.where(lane == d, 1.0, 0.0))
            v_hi = jnp.where(is_lo, jnp.where(lane == 0, 1.0, 0.0), b)
            return v_lo, v_hi

        kk = jnp.concatenate([keys(krp), keys(krc)], axis=0).astype(bf16)
        vlo_p, vhi_p = vals(vp)
        vlo_c, vhi_c = vals(vc)
        v_lo = jnp.concatenate([vlo_p, vlo_c], axis=0).astype(bf16)
        v_hi = jnp.concatenate([vhi_p, vhi_c], axis=0).astype(bf16)

        q = q_ref[:, g * gw:(g + 1) * gw].astype(f32)
        qr = rms_rot(q, qg_ref[...], gmat_ref[...], cosc, sinc) * (d ** -0.5)

        for p in range(pairs):
            qp = qr[:, p * LANES:(p + 1) * LANES]
            lhs = jnp.concatenate([jnp.where(is_lo, qp, 0.0), jnp.where(is_lo, 0.0, qp)], axis=0).astype(bf16)
            lg = lax.dot_general(lhs, kk, (((1,), (1,)), ((), ())), preferred_element_type=f32)
            merged = jnp.where(tri, lg[:, W:], lg[:, :W] + neg_prev)
            h0 = g * SWA_GROUP + 2 * p
            sink = jnp.where(top, sink_ref[h0], sink_ref[h0 + 1])
            m = jnp.maximum(jnp.max(merged, axis=-1, keepdims=True), sink)
            eb = jnp.exp(merged - m).astype(bf16)
            zero = jnp.zeros_like(eb)
            probs = jnp.concatenate([jnp.where(tri, zero, eb), jnp.where(tri, eb, zero)], axis=1)
            o_lo = jnp.dot(probs[:W], v_lo, preferred_element_type=f32)
            o_hi = jnp.dot(probs[W:], v_hi, preferred_element_type=f32)
            esink = jnp.exp(sink - m)
            r_lo = 1.0 / (o_lo[:, d:d + 1] + esink[:W])
            r_hi = 1.0 / (o_hi[:, 0:1] + esink[W:])
            out = jnp.where(is_lo, o_lo * r_lo, o_hi * r_hi)
            o_ref[:, g * gw + p * LANES:g * gw + (p + 1) * LANES] = out.astype(o_ref.dtype)


def _swa(proj, cos_s, sin_s, sinks, q_gain, k_gain, B, S, q_off, n_heads):
    T = proj.shape[0]
    W = WINDOW
    NB = S // W
    d = SWA_HEAD_DIM
    n_kv = n_heads // SWA_GROUP
    SW, KV = n_heads * d, n_kv * d
    gw = SWA_GROUP * d
    assert q_off % SW == 0 and (q_off + SW) % KV == 0 and KV % LANES == 0 and KV <= gw
    qg = jnp.tile(q_gain.astype(f32), SWA_GROUP)[None, :]
    kg = jnp.tile(k_gain.astype(f32), n_kv)[None, :]
    head_id = jnp.arange(gw) // d
    gmat = (head_id[:, None] == head_id[None, :]).astype(bf16)
    qb = q_off // SW
    kb = (q_off + SW) // KV

    def im(f):
        return lambda b, n, s: f(b, n)

    cur = lambda w, j: pl.BlockSpec((W, w), im(lambda b, n: (b * NB + n, j)))
    prev = lambda w, j: pl.BlockSpec((W, w), im(lambda b, n: (b * NB + jnp.maximum(n - 1, 0), j)))
    cst = lambda r, w: pl.BlockSpec((r, w), im(lambda b, n: (0, 0)))
    grid_spec = pltpu.PrefetchScalarGridSpec(
        num_scalar_prefetch=1,
        grid=(B, NB),
        in_specs=[cur(SW, qb), prev(KV, kb), cur(KV, kb), prev(KV, kb + 1), cur(KV, kb + 1),
                  prev(LANES, 0), prev(LANES, 0), cur(LANES, 0), cur(LANES, 0),
                  cst(1, gw), cst(1, KV), cst(gw, gw)],
        out_specs=cur(SW, 0),
    )
    return pl.pallas_call(
        _swa_kernel,
        grid_spec=grid_spec,
        out_shape=jax.ShapeDtypeStruct((T, SW), bf16),
        compiler_params=_cparams(("parallel", "parallel")),
        name="swa",
    )(sinks.astype(f32), proj, proj, proj, proj, proj, cos_s, sin_s, cos_s, sin_s, qg, kg, gmat)


def _out_proj_kernel(x_ref, a_ref, b_ref, wa_ref, wb_ref, o_ref):
    acc = jnp.dot(a_ref[...], wa_ref[...], preferred_element_type=f32)
    acc = acc + jnp.dot(b_ref[...], wb_ref[...], preferred_element_type=f32)
    o_ref[...] = x_ref[...] + acc


def _out_proj(x2d, ret, swa, w):
    T, D = x2d.shape
    R, SW = ret.shape[1], swa.shape[1]
    assert R == SW
    tm = _pick(T, (1024, 512, 256))
    tn = _pick(D, (512, 256, 128))
    return pl.pallas_call(
        _out_proj_kernel,
        grid=(T // tm, D // tn),
        in_specs=[pl.BlockSpec((tm, tn), lambda i, j: (i, j)),
                  pl.BlockSpec((tm, R), lambda i, j: (i, 0)),
                  pl.BlockSpec((tm, SW), lambda i, j: (i, 0)),
                  pl.BlockSpec((R, tn), lambda i, j: (0, j)),
                  pl.BlockSpec((SW, tn), lambda i, j: (1, j))],
        out_specs=pl.BlockSpec((tm, tn), lambda i, j: (i, j)),
        out_shape=jax.ShapeDtypeStruct((T, D), f32),
        compiler_params=_cparams(("parallel", "arbitrary")),
        name="out_proj",
    )(x2d, ret, swa, w, w)


def _router_kernel(h_ref, g_ref, wcat_ref, whi_ref, br_ref, hp_ref, idx_ref, gate_ref):
    h = h_ref[...]
    hn = h * lax.rsqrt(jnp.mean(h * h, axis=-1, keepdims=True) + NORM_EPS) * g_ref[...]
    E = whi_ref.shape[1]
    hn_hi = hn.astype(bf16)
    hn_hi32 = hn_hi.astype(f32)
    hn_lo = (hn - hn_hi32).astype(bf16)
    l1 = jnp.dot(hn_hi, wcat_ref[...], preferred_element_type=f32)
    l2 = jnp.dot(hn_lo, whi_ref[...], preferred_element_type=f32)
    logits = l1[:, :E] + l1[:, E:] + l2 + br_ref[...]
    lane = lax.broadcasted_iota(jnp.int32, logits.shape, 1)
    vals, idxs = [], []
    cur = logits
    for _ in range(TOP_K):
        m = jnp.max(cur, axis=-1, keepdims=True)
        i = jnp.min(jnp.where(cur == m, lane, E), axis=-1, keepdims=True)
        vals.append(m)
        idxs.append(i)
        cur = jnp.where(lane == i, -jnp.inf, cur)
    ex = [jnp.exp(v - vals[0]) for v in vals]
    tot = ex[0]
    for e in ex[1:]:
        tot = tot + e
    col = lax.broadcasted_iota(jnp.int32, idx_ref.shape, 1)
    idx_out = jnp.zeros(idx_ref.shape, jnp.int32)
    gate_out = jnp.zeros(gate_ref.shape, f32)
    for k in range(TOP_K):
        idx_out = jnp.where(col == k, idxs[k], idx_out)
        gate_out = jnp.where(col == k, ex[k] / tot, gate_out)
    idx_ref[...] = idx_out
    gate_ref[...] = gate_out
    half = hn.shape[1] // 2
    bits = lax.bitcast_convert_type(hn_hi32, u32)
    hp_ref[...] = bits[:, half:] | (bits[:, :half] >> 16)


def _router(h2d, gain, w_router, b_router):
    T, D = h2d.shape
    E = w_router.shape[1]
    w_hi = w_router.astype(bf16)
    w_lo = (w_router - w_hi.astype(f32)).astype(bf16)
    wcat = jnp.concatenate([w_hi, w_lo], axis=1)
    tr = min(256, T)
    row = lambda w: pl.BlockSpec((tr, w), lambda i: (i, 0))
    cst = lambda r, w: pl.BlockSpec((r, w), lambda i: (0, 0))
    return pl.pallas_call(
        _router_kernel,
        grid=(T // tr,),
        in_specs=[row(D), cst(1, D), cst(D, 2 * E), cst(D, E), cst(1, E)],
        out_specs=[row(D // 2), row(TOP_K), row(TOP_K)],
        out_shape=[jax.ShapeDtypeStruct((T, D // 2), u32),
                   jax.ShapeDtypeStruct((T, TOP_K), jnp.int32),
                   jax.ShapeDtypeStruct((T, TOP_K), f32)],
        compiler_params=_cparams(("parallel",)),
        name="ffn_norm_router",
    )(h2d, gain[None, :], wcat, w_hi, b_router[None, :])


def _moe_kernel(te_ref, tr_ref, rowtok_hbm, hp_hbm, win_hbm, wout_hbm,
                bg_ref, bl_ref, boa_ref, bob_ref, o_ref,
                xg_ref, act_ref, hm_ref, wbuf_ref, idx_ref, idx_sem, row_sem, w_sem,
                *, n_steps_in, n_steps_out, row_variants):
    i = pl.program_id(0)
    j = pl.program_id(1)
    n_tiles = pl.num_programs(0)
    tm = act_ref.shape[1]
    F = act_ref.shape[0] * act_ref.shape[2]
    half = xg_ref.shape[2]
    n_buf, D, tn = wbuf_ref.shape[0], wbuf_ref.shape[1], wbuf_ref.shape[2] // 2
    n_steps = n_steps_in + n_steps_out
    rows = tr_ref[i]
    rows_prev = tr_ref[jnp.maximum(i - 1, 0)]
    nxt = jnp.minimum(i + 1, n_tiles - 1)
    per_step = tm // n_steps_out

    def w_in_copy(e, ji, slot, c):
        return pltpu.make_async_copy(win_hbm.at[e, :, pl.ds(pl.multiple_of(c * F + ji * tn, tn), tn)],
                                     wbuf_ref.at[slot, :, pl.ds(c * tn, tn)], w_sem.at[slot])

    def w_out_copy(e, jo, slot, c):
        return pltpu.make_async_copy(wout_hbm.at[e, :, pl.ds(pl.multiple_of(c * half + jo * tn, tn), tn)],
                                     wbuf_ref.at[slot, pl.ds(0, F), pl.ds(c * tn, tn)], w_sem.at[slot])

    def w_step(p, start):
        it = p // n_steps
        jt = p - it * n_steps
        it_c = jnp.minimum(it, n_tiles - 1)

        @pl.when((it < n_tiles) & (tr_ref[it_c] > 0))
        def _():
            e = te_ref[it_c]
            slot = p % n_buf

            @pl.when(jt < n_steps_in)
            def _():
                for c in range(2):
                    cp = w_in_copy(e, jt, slot, c)
                    cp.start() if start else cp.wait()

            @pl.when(jt >= n_steps_in)
            def _():
                for c in range(2):
                    cp = w_out_copy(e, jt - n_steps_in, slot, c)
                    cp.start() if start else cp.wait()

    p = i * n_steps + j

    @pl.when(p == 0)
    def _():
        for q in range(n_buf - 1):
            w_step(q, True)

    w_step(p + n_buf - 1, True)
    w_step(p, False)
    slot = p % n_buf

    def idx_copy(t):
        return pltpu.make_async_copy(rowtok_hbm.at[t], idx_ref, idx_sem)

    def row_copy(r, g, u):
        tok = idx_ref[r]
        return pltpu.make_async_copy(hp_hbm.at[tok >> 3, tok & 7], xg_ref.at[g, u], row_sem)

    def gather_wait():
        pltpu.make_async_copy(hp_hbm.at[pl.ds(0, tm // SUBLANES)], xg_ref, row_sem).wait()

    @pl.when(j == 0)
    def _():
        @pl.when(i == 0)
        def _():
            idx_copy(0).start()
            idx_copy(0).wait()

            def body(g, carry):
                for u in range(SUBLANES):
                    row_copy(g * SUBLANES + u, g, u).start()
                return carry
            lax.fori_loop(0, tm // SUBLANES, body, 0)

        @pl.when((i == 0) | (rows_prev > 0))
        def _():
            gather_wait()

        @pl.when(rows > 0)
        def _():
            idx_copy(nxt).start()

    @pl.when((j == n_steps_in - 1) & (rows > 0))
    def _():
        idx_copy(nxt).wait()

    def activation(hm):
        hg = jnp.minimum(hm[:, :tn], SWIGLU_LIMIT)
        hl = jnp.clip(hm[:, tn:], -SWIGLU_LIMIT, SWIGLU_LIMIT)
        return (hg * jax.nn.sigmoid(SWIGLU_ALPHA * hg) * (hl + 1.0)).astype(bf16)

    @pl.when(p == 0)
    def _():
        hm_ref[...] = jnp.zeros_like(hm_ref)

    lo = 0
    for m in row_variants:
        in_bucket = (rows > lo) & (rows <= m)
        lo = m

        @pl.when((j < n_steps_in) & in_bucket)
        def _():
            act_ref[jnp.maximum(j - 1, 0), :m, :] = activation(hm_ref[:m, :])
            w = xg_ref[:m // SUBLANES].reshape(m, half)
            x = jnp.concatenate([_unpack_lo(w).astype(bf16), _unpack_hi(w).astype(bf16)], axis=1)
            hm_ref[:m, :tn] = jnp.dot(x, wbuf_ref[slot, :, :tn].astype(bf16), preferred_element_type=f32) + bg_ref[0]
            hm_ref[:m, tn:] = jnp.dot(x, wbuf_ref[slot, :, tn:].astype(bf16), preferred_element_type=f32) + bl_ref[0]

        @pl.when((j == n_steps_in) & in_bucket)
        def _():
            act_ref[n_steps_in - 1, :m, :] = activation(hm_ref[:m, :])

        @pl.when((j >= n_steps_in) & in_bucket)
        def _():
            jo = j - n_steps_in
            for r in range(per_step):
                row_copy(jo * per_step + r, jo * (per_step // SUBLANES) + r // SUBLANES, r % SUBLANES).start()
            a = jnp.concatenate([act_ref[k, :m, :] for k in range(n_steps_in)], axis=1)
            oa = jnp.dot(a, wbuf_ref[slot, :F, :tn].astype(bf16), preferred_element_type=f32) + boa_ref[0]
            ob = jnp.dot(a, wbuf_ref[slot, :F, tn:].astype(bf16), preferred_element_type=f32) + bob_ref[0]
            o_ref[:m, :] = _pack_bf16_pair(oa, ob)
            if m < tm:
                o_ref[m:, :] = jnp.zeros((tm - m, tn), u32)

    @pl.when((j >= n_steps_in) & (rows == 0))
    def _():
        o_ref[...] = jnp.zeros_like(o_ref)

    @pl.when((i == n_tiles - 1) & (j == n_steps_in + n_steps_out - 1) & (rows > 0))
    def _():
        gather_wait()


def _moe_tile_rows(T, n_experts):
    avg = T * TOP_K / n_experts
    passes = max(1, -(-int(avg) // MOE_TILE_TARGET))
    return -(-int(avg / passes * MOE_TILE_SLACK) // 64) * 64


def _moe(hp, tile_expert, tile_rows, row_tok, w_in, b_in, w_out, b_out):
    T, half = hp.shape
    D = 2 * half
    E, F = w_out.shape[0], w_out.shape[1]
    n_tiles, tm = row_tok.shape
    tn1 = _pick(F, (256, 128))
    tn2 = _pick(half, (256, 128))
    J1, J2 = F // tn1, half // tn2
    row_variants = (tm // 2, tm)
    assert tm % (J2 * SUBLANES) == 0 and (tm // 2) % (2 * SUBLANES) == 0
    assert tn1 == tn2 and F <= D and w_in.shape == (E, D, 2 * F) and w_out.shape == (E, F, D)

    def im(f):
        return lambda i, j, te, tr: f(i, j, te)

    def c1(j):
        return jnp.minimum(j, J1 - 1)

    def c2(j):
        return jnp.maximum(j - J1, 0)

    grid_spec = pltpu.PrefetchScalarGridSpec(
        num_scalar_prefetch=2,
        grid=(n_tiles, J1 + J2),
        in_specs=[
            pl.BlockSpec(memory_space=pl.ANY),
            pl.BlockSpec(memory_space=pl.ANY),
            pl.BlockSpec(memory_space=pl.ANY),
            pl.BlockSpec(memory_space=pl.ANY),
            pl.BlockSpec((1, 1, tn1), im(lambda i, j, te: (te[i], 0, c1(j)))),
            pl.BlockSpec((1, 1, tn1), im(lambda i, j, te: (te[i], 0, J1 + c1(j)))),
            pl.BlockSpec((1, 1, tn2), im(lambda i, j, te: (te[i], 0, c2(j)))),
            pl.BlockSpec((1, 1, tn2), im(lambda i, j, te: (te[i], 0, J2 + c2(j)))),
        ],
        out_specs=pl.BlockSpec((tm, tn2), im(lambda i, j, te: (i, c2(j)))),
        scratch_shapes=[
            pltpu.VMEM((tm // SUBLANES, SUBLANES, half), u32),
            pltpu.VMEM((J1, tm, tn1), bf16),
            pltpu.VMEM((tm, 2 * tn1), f32),
            pltpu.VMEM((MOE_WEIGHT_BUFFERS, D, 2 * tn1), f32),
            pltpu.SMEM((tm,), jnp.int32),
            pltpu.SemaphoreType.DMA,
            pltpu.SemaphoreType.DMA,
            pltpu.SemaphoreType.DMA((MOE_WEIGHT_BUFFERS,)),
        ],
    )
    b_in3, b_out3 = b_in[:, None, :], b_out[:, None, :]
    return pl.pallas_call(
        functools.partial(_moe_kernel, n_steps_in=J1, n_steps_out=J2, row_variants=row_variants),
        grid_spec=grid_spec,
        out_shape=jax.ShapeDtypeStruct((n_tiles * tm, half), u32),
        compiler_params=_cparams(("arbitrary", "arbitrary")),
        name="moe_experts",
    )(tile_expert, tile_rows, row_tok, hp.reshape(T // SUBLANES, SUBLANES, half),
      w_in, w_out, b_in3, b_in3, b_out3, b_out3)


def _combine_kernel(pos_hbm, o_hbm, h_ref, gate_ref, out_ref, buf_ref, idx0_ref, idx1_ref, idx_sem, row_sem):
    i = pl.program_id(0)
    n = pl.num_programs(0)
    tq, D = h_ref.shape
    half = D // 2
    qg = tq // SUBLANES
    idx_refs = (idx0_ref, idx1_ref)

    def idx_copy(t, s):
        return pltpu.make_async_copy(pos_hbm.at[t], idx_refs[s], idx_sem.at[s])

    def row_copy(s, g, u, k):
        r = idx_refs[s][(g * SUBLANES + u) * TOP_K + k]
        return pltpu.make_async_copy(o_hbm.at[r >> 3, r & 7], buf_ref.at[s, k * qg + g, u], row_sem.at[s])

    def gather_start(s):
        def body(g, carry):
            for u in range(SUBLANES):
                for k in range(TOP_K):
                    row_copy(s, g, u, k).start()
            return carry
        lax.fori_loop(0, qg, body, 0)

    def rows_wait(s):
        pltpu.make_async_copy(o_hbm.at[pl.ds(0, TOP_K * qg)], buf_ref.at[s], row_sem.at[s]).wait()

    nxt = jnp.minimum(i + 1, n - 1)

    @pl.when(i == 0)
    def _():
        idx_copy(0, 0).start()
        idx_copy(0, 0).wait()
        gather_start(0)
        idx_copy(nxt, 1).start()

    for s in range(2):
        @pl.when(i % 2 == s)
        def _():
            idx_copy(nxt, 1 - s).wait()
            rows_wait(s)
            gather_start(1 - s)
            gate = gate_ref[...]
            acc_lo = h_ref[:, :half]
            acc_hi = h_ref[:, half:]
            for k in range(TOP_K):
                w = buf_ref[s, k * qg:(k + 1) * qg].reshape(tq, half)
                gk = gate[:, k:k + 1]
                acc_lo = acc_lo + gk * _unpack_lo(w)
                acc_hi = acc_hi + gk * _unpack_hi(w)
            out_ref[:, :half] = acc_lo
            out_ref[:, half:] = acc_hi

            @pl.when(i + 1 < n)
            def _():
                idx_copy(jnp.minimum(i + 2, n - 1), s).start()

            @pl.when(i + 1 == n)
            def _():
                rows_wait(1 - s)


def _combine(h2d, gates, o_packed, pos):
    T, D = h2d.shape
    half = D // 2
    P = o_packed.shape[0]
    tq = min(128, T)
    pos2 = pos.reshape(T // tq, tq * TOP_K)
    return pl.pallas_call(
        _combine_kernel,
        grid=(T // tq,),
        in_specs=[pl.BlockSpec(memory_space=pl.ANY), pl.BlockSpec(memory_space=pl.ANY),
                  pl.BlockSpec((tq, D), lambda i: (i, 0)), pl.BlockSpec((tq, TOP_K), lambda i: (i, 0))],
        out_specs=pl.BlockSpec((tq, D), lambda i: (i, 0)),
        out_shape=jax.ShapeDtypeStruct((T, D), f32),
        scratch_shapes=[pltpu.VMEM((2, TOP_K * tq // SUBLANES, SUBLANES, half), u32),
                        pltpu.SMEM((TOP_K * tq,), jnp.int32),
                        pltpu.SMEM((TOP_K * tq,), jnp.int32),
                        pltpu.SemaphoreType.DMA((2,)),
                        pltpu.SemaphoreType.DMA((2,))],
        compiler_params=_cparams(("arbitrary",)),
        name="moe_combine",
    )(pos2, o_packed.reshape(P // SUBLANES, SUBLANES, half), h2d, gates)


def _routing_tables(top_idx, n_experts, tm):
    T = top_idx.shape[0]
    M = T * TOP_K
    e_flat = top_idx.reshape(-1)
    ids = jnp.arange(M, dtype=jnp.int32)
    _, order = lax.sort((e_flat, ids), num_keys=1, is_stable=True)
    _, inv = lax.sort((order, ids), num_keys=1)
    experts = jnp.arange(n_experts, dtype=jnp.int32)
    counts = jnp.sum((e_flat[:, None] == experts[None, :]).astype(jnp.int32), axis=0)
    starts = jnp.cumsum(counts) - counts
    tiles_per = (counts + tm - 1) // tm
    tile_end = jnp.cumsum(tiles_per)
    tile_start = tile_end - tiles_per
    rank = inv - starts[e_flat]
    pos = (tile_start[e_flat] + rank // tm) * tm + rank % tm
    n_tiles = M // tm + n_experts
    t_ids = jnp.arange(n_tiles, dtype=jnp.int32)
    te = jnp.minimum(jnp.sum((tile_end[None, :] <= t_ids[:, None]).astype(jnp.int32), axis=1), n_experts - 1)
    n_used = tile_end[-1]
    used = t_ids < n_used
    tile_expert = jnp.where(used, te, te[jnp.maximum(n_used - 1, 0)])
    k_in = t_ids - tile_start[te]
    tile_rows = jnp.where(used, jnp.clip(counts[te] - k_in * tm, 0, tm), 0).astype(jnp.int32)
    src = jnp.where(used, starts[te] + k_in * tm, 0)
    tok_sorted = jnp.concatenate([order // TOP_K, jnp.zeros((tm,), jnp.int32)])
    row_tok = jnp.take(tok_sorted, src[:, None] + jnp.arange(tm, dtype=jnp.int32)[None, :], axis=0)
    return tile_expert.astype(jnp.int32), tile_rows, row_tok, pos.reshape(T, TOP_K)


def kernel(x, positions, attn_norm_gain, w_in_proj, swa_q_gain, swa_k_gain, swa_sinks, w_out_proj, ffn_norm_gain,
           w_router, b_router, w_expert_in, b_expert_in, w_expert_out, b_expert_out):
    B, S, D = x.shape
    T = B * S
    depth = w_in_proj.shape[0]
    R = D // 2
    SW = D - R
    n_ret = R // RET_HEAD_DIM
    n_swa = swa_sinks.shape[1]
    KV = (n_swa // SWA_GROUP) * SWA_HEAD_DIM
    n_experts = w_router.shape[2]
    assert w_in_proj.shape[2] == 4 * R + SW + 2 * KV and SW == n_swa * SWA_HEAD_DIM and S % WINDOW == 0

    cos_r, sin_r, cos_s, sin_s = _rope_tables(positions.reshape(T, 1))
    h = x.reshape(T, D)
    for layer in range(depth):
        xn = _rmsnorm_bf16(h, attn_norm_gain[layer])
        proj = _in_proj(xn, w_in_proj[layer].astype(bf16))
        ret = _retention(proj, cos_r, sin_r, B, S, n_ret)
        swa = _swa(proj, cos_s, sin_s, swa_sinks[layer], swa_q_gain[layer], swa_k_gain[layer], B, S, 4 * R, n_swa)
        h = _out_proj(h, ret, swa, w_out_proj[layer].astype(bf16))

        hp, top_idx, gates = _router(h, ffn_norm_gain[layer], w_router[layer], b_router[layer])
        tm = _moe_tile_rows(T, n_experts)
        tile_expert, tile_rows, row_tok, pos = _routing_tables(top_idx, n_experts, tm)
        o_packed = _moe(hp, tile_expert, tile_rows, row_tok,
                        w_expert_in[layer], b_expert_in[layer], w_expert_out[layer], b_expert_out[layer])
        h = _combine(h, gates, o_packed, pos)
    return h.reshape(B, S, D)
```

```python
import functools

import jax
import jax.numpy as jnp
from jax import lax
from jax.experimental import pallas as pl
from jax.experimental.pallas import tpu as pltpu

RET_HEAD_DIM = 256
SWA_HEAD_DIM = 64
SWA_GROUP = 8
WINDOW = 128
ROPE_THETA = 10000.0
TOP_K = 4
SWIGLU_LIMIT = 7.0
SWIGLU_ALPHA = 1.702
NORM_EPS = 1e-5
NEG_INF = -1e30

RET_CHUNK = 256
MOE_TILE_TARGET = 1024
MOE_TILE_SLACK = 1.0625
MOE_WEIGHT_BUFFERS = 3
LANES = 128
SUBLANES = 8
VMEM_LIMIT = 56 * 1024 * 1024

f32 = jnp.float32
bf16 = jnp.bfloat16
u32 = jnp.uint32


def _cparams(sem, vmem=VMEM_LIMIT):
    return pltpu.CompilerParams(dimension_semantics=sem, vmem_limit_bytes=vmem)


def _pick(n, prefs):
    for p in prefs:
        if n % p == 0:
            return p
    return n


def _pack_bf16_pair(lo, hi):
    lo_b = lax.bitcast_convert_type(lo.astype(bf16).astype(f32), u32)
    hi_b = lax.bitcast_convert_type(hi.astype(bf16).astype(f32), u32)
    return hi_b | (lo_b >> 16)


def _unpack_lo(w):
    return lax.bitcast_convert_type(w << 16, f32)


def _unpack_hi(w):
    return lax.bitcast_convert_type(w & jnp.uint32(0xFFFF0000), f32)


def _rope_kernel(pos_ref, fr_ref, fs_ref, ss_ref, cr_ref, snr_ref, cs_ref, sns_ref):
    pos = pos_ref[...].astype(f32)
    ang_r = pos * fr_ref[...]
    cr_ref[...] = jnp.cos(ang_r)
    snr_ref[...] = jnp.sin(ang_r)
    ang_s = pos * fs_ref[...]
    cs_ref[...] = jnp.cos(ang_s)
    sns_ref[...] = jnp.sin(ang_s) * ss_ref[...]


def _rope_tables(pos_flat):
    T = pos_flat.shape[0]
    half = RET_HEAD_DIM // 2
    assert half == LANES
    fr = (1.0 / (ROPE_THETA ** jnp.linspace(0.0, 1.0, half, dtype=f32)))[None, :]
    d = SWA_HEAD_DIM
    inv_s = ROPE_THETA ** (-jnp.arange(0, d, 2, dtype=f32) / d)
    fs = jnp.tile(inv_s, LANES // (d // 2))[None, :]
    sgn = jnp.concatenate([-jnp.ones((d // 2,), f32), jnp.ones((d // 2,), f32)])
    ss = jnp.tile(sgn, LANES // d)[None, :]
    tr = min(512, T)
    row = lambda w: pl.BlockSpec((tr, w), lambda i: (i, 0))
    cst = pl.BlockSpec((1, LANES), lambda i: (0, 0))
    return pl.pallas_call(
        _rope_kernel,
        grid=(T // tr,),
        in_specs=[row(1), cst, cst, cst],
        out_specs=[row(LANES)] * 4,
        out_shape=[jax.ShapeDtypeStruct((T, LANES), f32)] * 4,
        compiler_params=_cparams(("parallel",)),
        name="rope_tables",
    )(pos_flat, fr, fs, ss)


def _rmsnorm_kernel(x_ref, g_ref, o_ref):
    x = x_ref[...]
    y = x * lax.rsqrt(jnp.mean(x * x, axis=-1, keepdims=True) + NORM_EPS)
    o_ref[...] = (y * g_ref[...]).astype(o_ref.dtype)


def _rmsnorm_bf16(x2d, gain):
    T, D = x2d.shape
    tr = min(256, T)
    return pl.pallas_call(
        _rmsnorm_kernel,
        grid=(T // tr,),
        in_specs=[pl.BlockSpec((tr, D), lambda i: (i, 0)), pl.BlockSpec((1, D), lambda i: (0, 0))],
        out_specs=pl.BlockSpec((tr, D), lambda i: (i, 0)),
        out_shape=jax.ShapeDtypeStruct((T, D), bf16),
        compiler_params=_cparams(("parallel",)),
        name="attn_rmsnorm",
    )(x2d, gain[None, :])


def _matmul_kernel(x_ref, w_ref, o_ref):
    o_ref[...] = jnp.dot(x_ref[...], w_ref[...], preferred_element_type=f32).astype(o_ref.dtype)


def _in_proj(xn, w):
    T, D = xn.shape
    N = w.shape[1]
    tm = _pick(T, (1024, 512, 256))
    tn = _pick(N, (1536, 768, 512, 256, 128))
    return pl.pallas_call(
        _matmul_kernel,
        grid=(T // tm, N // tn),
        in_specs=[pl.BlockSpec((tm, D), lambda i, j: (i, 0)), pl.BlockSpec((D, tn), lambda i, j: (0, j))],
        out_specs=pl.BlockSpec((tm, tn), lambda i, j: (i, j)),
        out_shape=jax.ShapeDtypeStruct((T, N), bf16),
        compiler_params=_cparams(("parallel", "arbitrary")),
        name="in_proj",
    )(xn, w)


def _retention_kernel(q_ref, k_ref, v_ref, g_ref, cos_ref, sin_ref, dec_ref, xi_ref, zeta_ref, cd_ref,
                      o_ref, state_ref):
    n = pl.program_id(1)
    hd = RET_HEAD_DIM
    n_heads = q_ref.shape[1] // hd

    @pl.when(n == 0)
    def _():
        state_ref[...] = jnp.zeros_like(state_ref)

    c = cos_ref[...]
    s = sin_ref[...]
    cos = jnp.concatenate([c, c], axis=1)
    sin = jnp.concatenate([-s, s], axis=1)

    def rot(x):
        return x * cos + pltpu.roll(x, hd // 2, 1) * sin

    for h in range(n_heads):
        sl = slice(h * hd, (h + 1) * hd)
        qr = rot(q_ref[:, sl].astype(f32))
        kr = rot(k_ref[:, sl].astype(f32)) * (hd ** -0.5)
        v = v_ref[:, sl]
        scores = lax.dot_general(qr.astype(bf16), kr.astype(bf16), (((1,), (1,)), ((), ())),
                                 preferred_element_type=f32)
        scores = scores * dec_ref[h]
        intra = jnp.dot(scores.astype(bf16), v, preferred_element_type=f32)
        state = state_ref[h]
        cross = jnp.dot((qr * xi_ref[h]).astype(bf16), state.astype(bf16), preferred_element_type=f32)
        kz = (kr * zeta_ref[h]).astype(bf16)
        kv = lax.dot_general(kz, v, (((0,), (0,)), ((), ())), preferred_element_type=f32)
        state_ref[h] = state * cd_ref[h] + kv
        o = intra + cross
        o = o * lax.rsqrt(jnp.mean(o * o, axis=-1, keepdims=True) + NORM_EPS)
        g = g_ref[:, sl].astype(f32)
        o_ref[:, sl] = (o * (g * jax.nn.sigmoid(g))).astype(o_ref.dtype)


def _retention(proj, cos_r, sin_r, B, S, n_heads):
    T = proj.shape[0]
    C = min(RET_CHUNK, S)
    NC = S // C
    hd = RET_HEAD_DIM
    R = n_heads * hd
    log_gamma = jnp.log(1.0 - 2.0 ** (-5.0 - jnp.arange(n_heads, dtype=f32)))
    idx = jnp.arange(C, dtype=f32)
    rel = idx[:, None] - idx[None, :]
    decay = jnp.where(rel >= 0, jnp.exp(log_gamma[:, None, None] * jnp.maximum(rel, 0.0)), 0.0)
    xi = jnp.exp(log_gamma[:, None] * (idx + 1.0))[:, :, None]
    zeta = jnp.exp(log_gamma[:, None] * (C - 1.0 - idx))[:, :, None]
    cdec = jnp.exp(log_gamma * C)[:, None, None]

    col = lambda j: pl.BlockSpec((C, R), lambda b, n: (b * NC + n, j))
    tab = pl.BlockSpec((C, LANES), lambda b, n: (b * NC + n, 0))
    per_head = lambda s: pl.BlockSpec((n_heads,) + s, lambda b, n: (0, 0, 0))
    return pl.pallas_call(
        _retention_kernel,
        grid=(B, NC),
        in_specs=[col(0), col(1), col(2), col(3), tab, tab,
                  per_head((C, C)), per_head((C, 1)), per_head((C, 1)), per_head((1, 1))],
        out_specs=pl.BlockSpec((C, R), lambda b, n: (b * NC + n, 0)),
        out_shape=jax.ShapeDtypeStruct((T, R), bf16),
        scratch_shapes=[pltpu.VMEM((n_heads, hd, hd), f32)],
        compiler_params=_cparams(("parallel", "arbitrary")),
        name="retention",
    )(proj, proj, proj, proj, cos_r, sin_r, decay, xi, zeta, cdec)


def _swap32(x):
    w = x.shape[-1]
    lane = lax.broadcasted_iota(jnp.int32, x.shape, 1)
    first = (lane % SWA_HEAD_DIM) < (SWA_HEAD_DIM // 2)
    return jnp.where(first, pltpu.roll(x, w - SWA_HEAD_DIM // 2, 1), pltpu.roll(x, SWA_HEAD_DIM // 2, 1))


def _swa_kernel(sink_ref, q_ref, kp_ref, kc_ref, vp_ref, vc_ref, cosp_ref, sinp_ref, cosc_ref, sinc_ref,
                qg_ref, kg_ref, gmat_ref, o_ref):
    n = pl.program_id(1)
    W = WINDOW
    d = SWA_HEAD_DIM
    gw = SWA_GROUP * d
    kvw = kc_ref.shape[1]
    n_kv = kvw // d
    pairs = gw // LANES

    lane = lax.broadcasted_iota(jnp.int32, (W, LANES), 1)
    is_lo = lane < d
    cosc = cosc_ref[...]
    sinc = sinc_ref[...]

    def rms_rot(x, gain, gsub, cos, sin):
        reps = x.shape[1] // LANES
        ssq = jnp.dot((x * x).astype(bf16), gsub, preferred_element_type=f32)
        xn = x * lax.rsqrt(ssq * (1.0 / d) + NORM_EPS) * gain
        return xn * jnp.concatenate([cos] * reps, axis=1) + _swap32(xn) * jnp.concatenate([sin] * reps, axis=1)

    gk = gmat_ref[:kvw, :kvw]
    krp = rms_rot(kp_ref[...].astype(f32), kg_ref[...], gk, cosp_ref[...], sinp_ref[...])
    krc = rms_rot(kc_ref[...].astype(f32), kg_ref[...], gk, cosc, sinc)
    vp = vp_ref[...].astype(f32)
    vc = vc_ref[...].astype(f32)

    neg_prev = jnp.where(n > 0, 0.0, NEG_INF)
    row = lax.broadcasted_iota(jnp.int32, (2 * W, W), 0) % W
    colj = lax.broadcasted_iota(jnp.int32, (2 * W, W), 1)
    tri = colj <= row
    top = lax.broadcasted_iota(jnp.int32, (2 * W, 1), 0) < W

    for g in range(n_kv):
        c, hi_half = divmod(g, LANES // d)

        def halves(x):
            xc = x[:, c * LANES:(c + 1) * LANES]
            r = pltpu.roll(xc, d, 1)
            return (r, xc) if hi_half else (xc, r)

        def keys(x):
            a, b = halves(x)
            return jnp.where(is_lo, a, b)

        def vals(x):
            a, b = halves(x)
            v_lo = jnp.where(is_lo, a, jnp.where(lane == d, 1.0, 0.0))
            v_hi = jnp.where(is_lo, jnp.where(lane == 0, 1.0, 0.0), b)
            return v_lo, v_hi

        kk = jnp.concatenate([keys(krp), keys(krc)], axis=0).astype(bf16)
        vlo_p, vhi_p = vals(vp)
        vlo_c, vhi_c = vals(vc)
        v_lo = jnp.concatenate([vlo_p, vlo_c], axis=0).astype(bf16)
        v_hi = jnp.concatenate([vhi_p, vhi_c], axis=0).astype(bf16)

        q = q_ref[:, g * gw:(g + 1) * gw].astype(f32)
        qr = rms_rot(q, qg_ref[...], gmat_ref[...], cosc, sinc) * (d ** -0.5)

        for p in range(pairs):
            qp = qr[:, p * LANES:(p + 1) * LANES]
            lhs = jnp.concatenate([jnp.where(is_lo, qp, 0.0), jnp.where(is_lo, 0.0, qp)], axis=0).astype(bf16)
            lg = lax.dot_general(lhs, kk, (((1,), (1,)), ((), ())), preferred_element_type=f32)
            merged = jnp.where(tri, lg[:, W:], lg[:, :W] + neg_prev)
            h0 = g * SWA_GROUP + 2 * p
            sink = jnp.where(top, sink_ref[h0], sink_ref[h0 + 1])
            m = jnp.maximum(jnp.max(merged, axis=-1, keepdims=True), sink)
            eb = jnp.exp(merged - m).astype(bf16)
            zero = jnp.zeros_like(eb)
            probs = jnp.concatenate([jnp.where(tri, zero, eb), jnp.where(tri, eb, zero)], axis=1)
            o_lo = jnp.dot(probs[:W], v_lo, preferred_element_type=f32)
            o_hi = jnp.dot(probs[W:], v_hi, preferred_element_type=f32)
            esink = jnp.exp(sink - m)
            r_lo = 1.0 / (o_lo[:, d:d + 1] + esink[:W])
            r_hi = 1.0 / (o_hi[:, 0:1] + esink[W:])
            out = jnp.where(is_lo, o_lo * r_lo, o_hi * r_hi)
            o_ref[:, g * gw + p * LANES:g * gw + (p + 1) * LANES] = out.astype(o_ref.dtype)


def _swa(proj, cos_s, sin_s, sinks, q_gain, k_gain, B, S, q_off, n_heads):
    T = proj.shape[0]
    W = WINDOW
    NB = S // W
    d = SWA_HEAD_DIM
    n_kv = n_heads // SWA_GROUP
    SW, KV = n_heads * d, n_kv * d
    gw = SWA_GROUP * d
    assert q_off % SW == 0 and (q_off + SW) % KV == 0 and KV % LANES == 0 and KV <= gw
    qg = jnp.tile(q_gain.astype(f32), SWA_GROUP)[None, :]
    kg = jnp.tile(k_gain.astype(f32), n_kv)[None, :]
    head_id = jnp.arange(gw) // d
    gmat = (head_id[:, None] == head_id[None, :]).astype(bf16)
    qb = q_off // SW
    kb = (q_off + SW) // KV

    def im(f):
        return lambda b, n, s: f(b, n)

    cur = lambda w, j: pl.BlockSpec((W, w), im(lambda b, n: (b * NB + n, j)))
    prev = lambda w, j: pl.BlockSpec((W, w), im(lambda b, n: (b * NB + jnp.maximum(n - 1, 0), j)))
    cst = lambda r, w: pl.BlockSpec((r, w), im(lambda b, n: (0, 0)))
    grid_spec = pltpu.PrefetchScalarGridSpec(
        num_scalar_prefetch=1,
        grid=(B, NB),
        in_specs=[cur(SW, qb), prev(KV, kb), cur(KV, kb), prev(KV, kb + 1), cur(KV, kb + 1),
                  prev(LANES, 0), prev(LANES, 0), cur(LANES, 0), cur(LANES, 0),
                  cst(1, gw), cst(1, KV), cst(gw, gw)],
        out_specs=cur(SW, 0),
    )
    return pl.pallas_call(
        _swa_kernel,
        grid_spec=grid_spec,
        out_shape=jax.ShapeDtypeStruct((T, SW), bf16),
        compiler_params=_cparams(("parallel", "parallel")),
        name="swa",
    )(sinks.astype(f32), proj, proj, proj, proj, proj, cos_s, sin_s, cos_s, sin_s, qg, kg, gmat)


def _out_proj_kernel(x_ref, a_ref, b_ref, wa_ref, wb_ref, o_ref):
    acc = jnp.dot(a_ref[...], wa_ref[...], preferred_element_type=f32)
    acc = acc + jnp.dot(b_ref[...], wb_ref[...], preferred_element_type=f32)
    o_ref[...] = x_ref[...] + acc


def _out_proj(x2d, ret, swa, w):
    T, D = x2d.shape
    R, SW = ret.shape[1], swa.shape[1]
    assert R == SW
    tm = _pick(T, (1024, 512, 256))
    tn = _pick(D, (1024, 512, 256, 128))
    return pl.pallas_call(
        _out_proj_kernel,
        grid=(T // tm, D // tn),
        in_specs=[pl.BlockSpec((tm, tn), lambda i, j: (i, j)),
                  pl.BlockSpec((tm, R), lambda i, j: (i, 0)),
                  pl.BlockSpec((tm, SW), lambda i, j: (i, 0)),
                  pl.BlockSpec((R, tn), lambda i, j: (0, j)),
                  pl.BlockSpec((SW, tn), lambda i, j: (1, j))],
        out_specs=pl.BlockSpec((tm, tn), lambda i, j: (i, j)),
        out_shape=jax.ShapeDtypeStruct((T, D), f32),
        compiler_params=_cparams(("parallel", "arbitrary")),
        name="out_proj",
    )(x2d, ret, swa, w, w)


def _router_kernel(h_ref, g_ref, wcat_ref, whi_ref, br_ref, hp_ref, idx_ref, gate_ref):
    h = h_ref[...]
    hn = h * lax.rsqrt(jnp.mean(h * h, axis=-1, keepdims=True) + NORM_EPS) * g_ref[...]
    E = whi_ref.shape[1]
    hn_hi = hn.astype(bf16)
    hn_hi32 = hn_hi.astype(f32)
    hn_lo = (hn - hn_hi32).astype(bf16)
    l1 = jnp.dot(hn_hi, wcat_ref[...], preferred_element_type=f32)
    l2 = jnp.dot(hn_lo, whi_ref[...], preferred_element_type=f32)
    logits = l1[:, :E] + l1[:, E:] + l2 + br_ref[...]
    lane = lax.broadcasted_iota(jnp.int32, logits.shape, 1)
    vals, idxs = [], []
    cur = logits
    for _ in range(TOP_K):
        m = jnp.max(cur, axis=-1, keepdims=True)
        i = jnp.min(jnp.where(cur == m, lane, E), axis=-1, keepdims=True)
        vals.append(m)
        idxs.append(i)
        cur = jnp.where(lane == i, -jnp.inf, cur)
    ex = [jnp.exp(v - vals[0]) for v in vals]
    tot = ex[0]
    for e in ex[1:]:
        tot = tot + e
    col = lax.broadcasted_iota(jnp.int32, idx_ref.shape, 1)
    idx_out = jnp.zeros(idx_ref.shape, jnp.int32)
    gate_out = jnp.zeros(gate_ref.shape, f32)
    for k in range(TOP_K):
        idx_out = jnp.where(col == k, idxs[k], idx_out)
        gate_out = jnp.where(col == k, ex[k] / tot, gate_out)
    idx_ref[...] = idx_out
    gate_ref[...] = gate_out
    half = hn.shape[1] // 2
    bits = lax.bitcast_convert_type(hn_hi32, u32)
    hp_ref[...] = bits[:, half:] | (bits[:, :half] >> 16)


def _router(h2d, gain, w_router, b_router):
    T, D = h2d.shape
    E = w_router.shape[1]
    w_hi = w_router.astype(bf16)
    w_lo = (w_router - w_hi.astype(f32)).astype(bf16)
    wcat = jnp.concatenate([w_hi, w_lo], axis=1)
    tr = min(256, T)
    row = lambda w: pl.BlockSpec((tr, w), lambda i: (i, 0))
    cst = lambda r, w: pl.BlockSpec((r, w), lambda i: (0, 0))
    return pl.pallas_call(
        _router_kernel,
        grid=(T // tr,),
        in_specs=[row(D), cst(1, D), cst(D, 2 * E), cst(D, E), cst(1, E)],
        out_specs=[row(D // 2), row(TOP_K), row(TOP_K)],
        out_shape=[jax.ShapeDtypeStruct((T, D // 2), u32),
                   jax.ShapeDtypeStruct((T, TOP_K), jnp.int32),
                   jax.ShapeDtypeStruct((T, TOP_K), f32)],
        compiler_params=_cparams(("parallel",)),
        name="ffn_norm_router",
    )(h2d, gain[None, :], wcat, w_hi, b_router[None, :])


def _moe_kernel(te_ref, tr_ref, rowtok_hbm, hp_hbm, win_hbm, wout_hbm,
                bg_ref, bl_ref, boa_ref, bob_ref, o_ref,
                xg_ref, act_ref, hm_ref, wbuf_ref, idx_ref, idx_sem, row_sem, w_sem,
                *, n_steps_in, n_steps_out, row_variants):
    i = pl.program_id(0)
    j = pl.program_id(1)
    n_tiles = pl.num_programs(0)
    tm = act_ref.shape[1]
    F = act_ref.shape[0] * act_ref.shape[2]
    half = xg_ref.shape[2]
    n_buf, D, tn = wbuf_ref.shape[0], wbuf_ref.shape[1], wbuf_ref.shape[2] // 2
    n_steps = n_steps_in + n_steps_out
    rows = tr_ref[i]
    rows_prev = tr_ref[jnp.maximum(i - 1, 0)]
    nxt = jnp.minimum(i + 1, n_tiles - 1)
    per_step = tm // n_steps_out

    def w_in_copy(e, ji, slot, c):
        return pltpu.make_async_copy(win_hbm.at[e, :, pl.ds(pl.multiple_of(c * F + ji * tn, tn), tn)],
                                     wbuf_ref.at[slot, :, pl.ds(c * tn, tn)], w_sem.at[slot])

    def w_out_copy(e, jo, slot, c):
        return pltpu.make_async_copy(wout_hbm.at[e, :, pl.ds(pl.multiple_of(c * half + jo * tn, tn), tn)],
                                     wbuf_ref.at[slot, pl.ds(0, F), pl.ds(c * tn, tn)], w_sem.at[slot])

    def w_step(p, start):
        it = p // n_steps
        jt = p - it * n_steps
        it_c = jnp.minimum(it, n_tiles - 1)

        @pl.when((it < n_tiles) & (tr_ref[it_c] > 0))
        def _():
            e = te_ref[it_c]
            slot = p % n_buf

            @pl.when(jt < n_steps_in)
            def _():
                for c in range(2):
                    cp = w_in_copy(e, jt, slot, c)
                    cp.start() if start else cp.wait()

            @pl.when(jt >= n_steps_in)
            def _():
                for c in range(2):
                    cp = w_out_copy(e, jt - n_steps_in, slot, c)
                    cp.start() if start else cp.wait()

    p = i * n_steps + j

    @pl.when(p == 0)
    def _():
        for q in range(n_buf - 1):
            w_step(q, True)

    w_step(p + n_buf - 1, True)
    w_step(p, False)
    slot = p % n_buf

    def idx_copy(t):
        return pltpu.make_async_copy(rowtok_hbm.at[t], idx_ref, idx_sem)

    def row_copy(r, g, u):
        tok = idx_ref[r]
        return pltpu.make_async_copy(hp_hbm.at[tok >> 3, tok & 7], xg_ref.at[g, u], row_sem)

    def gather_wait():
        pltpu.make_async_copy(hp_hbm.at[pl.ds(0, tm // SUBLANES)], xg_ref, row_sem).wait()

    @pl.when(j == 0)
    def _():
        @pl.when(i == 0)
        def _():
            idx_copy(0).start()
            idx_copy(0).wait()

            def body(g, carry):
                for u in range(SUBLANES):
                    row_copy(g * SUBLANES + u, g, u).start()
                return carry
            lax.fori_loop(0, tm // SUBLANES, body, 0)

        @pl.when((i == 0) | (rows_prev > 0))
        def _():
            gather_wait()

        @pl.when(rows > 0)
        def _():
            idx_copy(nxt).start()

    @pl.when((j == n_steps_in - 1) & (rows > 0))
    def _():
        idx_copy(nxt).wait()

    def activation(hm):
        hg = jnp.minimum(hm[:, :tn], SWIGLU_LIMIT)
        hl = jnp.clip(hm[:, tn:], -SWIGLU_LIMIT, SWIGLU_LIMIT)
        return (hg * jax.nn.sigmoid(SWIGLU_ALPHA * hg) * (hl + 1.0)).astype(bf16)

    @pl.when(p == 0)
    def _():
        hm_ref[...] = jnp.zeros_like(hm_ref)

    lo = 0
    for m in row_variants:
        in_bucket = (rows > lo) & (rows <= m)
        lo = m

        @pl.when((j < n_steps_in) & in_bucket)
        def _():
            act_ref[jnp.maximum(j - 1, 0), :m, :] = activation(hm_ref[:m, :])
            w = xg_ref[:m // SUBLANES].reshape(m, half)
            x = jnp.concatenate([_unpack_lo(w).astype(bf16), _unpack_hi(w).astype(bf16)], axis=1)
            hm_ref[:m, :tn] = jnp.dot(x, wbuf_ref[slot, :, :tn].astype(bf16), preferred_element_type=f32) + bg_ref[0]
            hm_ref[:m, tn:] = jnp.dot(x, wbuf_ref[slot, :, tn:].astype(bf16), preferred_element_type=f32) + bl_ref[0]

        @pl.when((j == n_steps_in) & in_bucket)
        def _():
            act_ref[n_steps_in - 1, :m, :] = activation(hm_ref[:m, :])

        @pl.when((j >= n_steps_in) & in_bucket)
        def _():
            jo = j - n_steps_in
            for r in range(per_step):
                row_copy(jo * per_step + r, jo * (per_step // SUBLANES) + r // SUBLANES, r % SUBLANES).start()
            a = jnp.concatenate([act_ref[k, :m, :] for k in range(n_steps_in)], axis=1)
            oa = jnp.dot(a, wbuf_ref[slot, :F, :tn].astype(bf16), preferred_element_type=f32) + boa_ref[0]
            ob = jnp.dot(a, wbuf_ref[slot, :F, tn:].astype(bf16), preferred_element_type=f32) + bob_ref[0]
            o_ref[:m, :] = _pack_bf16_pair(oa, ob)
            if m < tm:
                o_ref[m:, :] = jnp.zeros((tm - m, tn), u32)

    @pl.when((j >= n_steps_in) & (rows == 0))
    def _():
        o_ref[...] = jnp.zeros_like(o_ref)

    @pl.when((i == n_tiles - 1) & (j == n_steps_in + n_steps_out - 1) & (rows > 0))
    def _():
        gather_wait()


def _moe_tile_rows(T, n_experts):
    avg = T * TOP_K / n_experts
    passes = max(1, -(-int(avg) // MOE_TILE_TARGET))
    return -(-int(avg / passes * MOE_TILE_SLACK) // 64) * 64


def _moe(hp, tile_expert, tile_rows, row_tok, w_in, b_in, w_out, b_out):
    T, half = hp.shape
    D = 2 * half
    E, F = w_out.shape[0], w_out.shape[1]
    n_tiles, tm = row_tok.shape
    tn1 = _pick(F, (256, 128))
    tn2 = _pick(half, (256, 128))
    J1, J2 = F // tn1, half // tn2
    row_variants = (tm // 2, tm)
    assert tm % (J2 * SUBLANES) == 0 and (tm // 2) % (2 * SUBLANES) == 0
    assert tn1 == tn2 and F <= D and w_in.shape == (E, D, 2 * F) and w_out.shape == (E, F, D)

    def im(f):
        return lambda i, j, te, tr: f(i, j, te)

    def c1(j):
        return jnp.minimum(j, J1 - 1)

    def c2(j):
        return jnp.maximum(j - J1, 0)

    grid_spec = pltpu.PrefetchScalarGridSpec(
        num_scalar_prefetch=2,
        grid=(n_tiles, J1 + J2),
        in_specs=[
            pl.BlockSpec(memory_space=pl.ANY),
            pl.BlockSpec(memory_space=pl.ANY),
            pl.BlockSpec(memory_space=pl.ANY),
            pl.BlockSpec(memory_space=pl.ANY),
            pl.BlockSpec((1, 1, tn1), im(lambda i, j, te: (te[i], 0, c1(j)))),
            pl.BlockSpec((1, 1, tn1), im(lambda i, j, te: (te[i], 0, J1 + c1(j)))),
            pl.BlockSpec((1, 1, tn2), im(lambda i, j, te: (te[i], 0, c2(j)))),
            pl.BlockSpec((1, 1, tn2), im(lambda i, j, te: (te[i], 0, J2 + c2(j)))),
        ],
        out_specs=pl.BlockSpec((tm, tn2), im(lambda i, j, te: (i, c2(j)))),
        scratch_shapes=[
            pltpu.VMEM((tm // SUBLANES, SUBLANES, half), u32),
            pltpu.VMEM((J1, tm, tn1), bf16),
            pltpu.VMEM((tm, 2 * tn1), f32),
            pltpu.VMEM((MOE_WEIGHT_BUFFERS, D, 2 * tn1), f32),
            pltpu.SMEM((tm,), jnp.int32),
            pltpu.SemaphoreType.DMA,
            pltpu.SemaphoreType.DMA,
            pltpu.SemaphoreType.DMA((MOE_WEIGHT_BUFFERS,)),
        ],
    )
    b_in3, b_out3 = b_in[:, None, :], b_out[:, None, :]
    return pl.pallas_call(
        functools.partial(_moe_kernel, n_steps_in=J1, n_steps_out=J2, row_variants=row_variants),
        grid_spec=grid_spec,
        out_shape=jax.ShapeDtypeStruct((n_tiles * tm, half), u32),
        compiler_params=_cparams(("arbitrary", "arbitrary")),
        name="moe_experts",
    )(tile_expert, tile_rows, row_tok, hp.reshape(T // SUBLANES, SUBLANES, half),
      w_in, w_out, b_in3, b_in3, b_out3, b_out3)


def _combine_kernel(pos_hbm, o_hbm, h_ref, gate_ref, out_ref, buf_ref, idx0_ref, idx1_ref, idx_sem, row_sem):
    i = pl.program_id(0)
    n = pl.num_programs(0)
    tq, D = h_ref.shape
    half = D // 2
    qg = tq // SUBLANES
    idx_refs = (idx0_ref, idx1_ref)

    def idx_copy(t, s):
        return pltpu.make_async_copy(pos_hbm.at[t], idx_refs[s], idx_sem.at[s])

    def row_copy(s, g, u, k):
        r = idx_refs[s][(g * SUBLANES + u) * TOP_K + k]
        return pltpu.make_async_copy(o_hbm.at[r >> 3, r & 7], buf_ref.at[s, k * qg + g, u], row_sem.at[s])

    def gather_start(s):
        def body(g, carry):
            for u in range(SUBLANES):
                for k in range(TOP_K):
                    row_copy(s, g, u, k).start()
            return carry
        lax.fori_loop(0, qg, body, 0)

    def rows_wait(s):
        pltpu.make_async_copy(o_hbm.at[pl.ds(0, TOP_K * qg)], buf_ref.at[s], row_sem.at[s]).wait()

    nxt = jnp.minimum(i + 1, n - 1)

    @pl.when(i == 0)
    def _():
        idx_copy(0, 0).start()
        idx_copy(0, 0).wait()
        gather_start(0)
        idx_copy(nxt, 1).start()

    for s in range(2):
        @pl.when(i % 2 == s)
        def _():
            idx_copy(nxt, 1 - s).wait()
            rows_wait(s)
            gather_start(1 - s)
            gate = gate_ref[...]
            acc_lo = h_ref[:, :half]
            acc_hi = h_ref[:, half:]
            for k in range(TOP_K):
                w = buf_ref[s, k * qg:(k + 1) * qg].reshape(tq, half)
                gk = gate[:, k:k + 1]
                acc_lo = acc_lo + gk * _unpack_lo(w)
                acc_hi = acc_hi + gk * _unpack_hi(w)
            out_ref[:, :half] = acc_lo
            out_ref[:, half:] = acc_hi

            @pl.when(i + 1 < n)
            def _():
                idx_copy(jnp.minimum(i + 2, n - 1), s).start()

            @pl.when(i + 1 == n)
            def _():
                rows_wait(1 - s)


def _combine(h2d, gates, o_packed, pos):
    T, D = h2d.shape
    half = D // 2
    P = o_packed.shape[0]
    tq = _pick(T, (256, 128))
    pos2 = pos.reshape(T // tq, tq * TOP_K)
    return pl.pallas_call(
        _combine_kernel,
        grid=(T // tq,),
        in_specs=[pl.BlockSpec(memory_space=pl.ANY), pl.BlockSpec(memory_space=pl.ANY),
                  pl.BlockSpec((tq, D), lambda i: (i, 0)), pl.BlockSpec((tq, TOP_K), lambda i: (i, 0))],
        out_specs=pl.BlockSpec((tq, D), lambda i: (i, 0)),
        out_shape=jax.ShapeDtypeStruct((T, D), f32),
        scratch_shapes=[pltpu.VMEM((2, TOP_K * tq // SUBLANES, SUBLANES, half), u32),
                        pltpu.SMEM((TOP_K * tq,), jnp.int32),
                        pltpu.SMEM((TOP_K * tq,), jnp.int32),
                        pltpu.SemaphoreType.DMA((2,)),
                        pltpu.SemaphoreType.DMA((2,))],
        compiler_params=_cparams(("arbitrary",)),
        name="moe_combine",
    )(pos2, o_packed.reshape(P // SUBLANES, SUBLANES, half), h2d, gates)


def _routing_tables(top_idx, n_experts, tm):
    T = top_idx.shape[0]
    M = T * TOP_K
    e_flat = top_idx.reshape(-1)
    ids = jnp.arange(M, dtype=jnp.int32)
    _, order = lax.sort((e_flat, ids), num_keys=1, is_stable=True)
    _, inv = lax.sort((order, ids), num_keys=1)
    experts = jnp.arange(n_experts, dtype=jnp.int32)
    counts = jnp.sum((e_flat[:, None] == experts[None, :]).astype(jnp.int32), axis=0)
    starts = jnp.cumsum(counts) - counts
    tiles_per = (counts + tm - 1) // tm
    tile_end = jnp.cumsum(tiles_per)
    tile_start = tile_end - tiles_per
    rank = inv - starts[e_flat]
    pos = (tile_start[e_flat] + rank // tm) * tm + rank % tm
    n_tiles = M // tm + n_experts
    t_ids = jnp.arange(n_tiles, dtype=jnp.int32)
    te = jnp.minimum(jnp.sum((tile_end[None, :] <= t_ids[:, None]).astype(jnp.int32), axis=1), n_experts - 1)
    n_used = tile_end[-1]
    used = t_ids < n_used
    tile_expert = jnp.where(used, te, te[jnp.maximum(n_used - 1, 0)])
    k_in = t_ids - tile_start[te]
    tile_rows = jnp.where(used, jnp.clip(counts[te] - k_in * tm, 0, tm), 0).astype(jnp.int32)
    src = jnp.where(used, starts[te] + k_in * tm, 0)
    tok_sorted = jnp.concatenate([order // TOP_K, jnp.zeros((tm,), jnp.int32)])
    row_tok = jnp.take(tok_sorted, src[:, None] + jnp.arange(tm, dtype=jnp.int32)[None, :], axis=0)
    return tile_expert.astype(jnp.int32), tile_rows, row_tok, pos.reshape(T, TOP_K)


def kernel(x, positions, attn_norm_gain, w_in_proj, swa_q_gain, swa_k_gain, swa_sinks, w_out_proj, ffn_norm_gain,
           w_router, b_router, w_expert_in, b_expert_in, w_expert_out, b_expert_out):
    B, S, D = x.shape
    T = B * S
    depth = w_in_proj.shape[0]
    R = D // 2
    SW = D - R
    n_ret = R // RET_HEAD_DIM
    n_swa = swa_sinks.shape[1]
    KV = (n_swa // SWA_GROUP) * SWA_HEAD_DIM
    n_experts = w_router.shape[2]
    assert w_in_proj.shape[2] == 4 * R + SW + 2 * KV and SW == n_swa * SWA_HEAD_DIM and S % WINDOW == 0

    cos_r, sin_r, cos_s, sin_s = _rope_tables(positions.reshape(T, 1))
    h = x.reshape(T, D)
    for layer in range(depth):
        xn = _rmsnorm_bf16(h, attn_norm_gain[layer])
        proj = _in_proj(xn, w_in_proj[layer].astype(bf16))
        ret = _retention(proj, cos_r, sin_r, B, S, n_ret)
        swa = _swa(proj, cos_s, sin_s, swa_sinks[layer], swa_q_gain[layer], swa_k_gain[layer], B, S, 4 * R, n_swa)
        h = _out_proj(h, ret, swa, w_out_proj[layer].astype(bf16))

        hp, top_idx, gates = _router(h, ffn_norm_gain[layer], w_router[layer], b_router[layer])
        tm = _moe_tile_rows(T, n_experts)
        tile_expert, tile_rows, row_tok, pos = _routing_tables(top_idx, n_experts, tm)
        o_packed = _moe(hp, tile_expert, tile_rows, row_tok,
                        w_expert_in[layer], b_expert_in[layer], w_expert_out[layer], b_expert_out[layer])
        h = _combine(h, gates, o_packed, pos)
    return h.reshape(B, S, D)
```
